```python
import jax, jax.numpy as jnp
from jax import lax
import numpy as np

D_MODEL = 1024
BATCH = 32
SEQ = 2048
DEPTH = 1

POOL_WIDTH = D_MODEL // 2
POOL_GROUPS = 4
POOL_GROUP_DIM = POOL_WIDTH // POOL_GROUPS
POOL_WINDOWS = (2, 4, 8, 16)
MAX_WINDOW = max(POOL_WINDOWS)
CONV_WIDTH = D_MODEL // 2
CONV_KERNEL = 31
N_BRANCHES = 2
IN_PROJ_DIM = POOL_WIDTH + 2 * CONV_WIDTH + N_BRANCHES * D_MODEL
PEER_HEADS = 8
PEER_N_KEYS = 128
PEER_N_EXPERTS = PEER_N_KEYS * PEER_N_KEYS
PEER_TOPK = 16
PEER_KEY_DIM = 256
PEER_HALF = PEER_KEY_DIM // 2
PEER_TOKEN_BLOCK = 128
EPS = 1e-6

kernel_name = "gated_pool_conformer_peer_block"


def rmsnorm(x, g):
    xf = x.astype(jnp.float32)
    y = xf * lax.rsqrt(jnp.mean(xf * xf, axis=-1, keepdims=True) + EPS)
    return (y * g.astype(jnp.float32)).astype(x.dtype)


def layernorm(x, g, b):
    xf = x.astype(jnp.float32)
    mu = jnp.mean(xf, axis=-1, keepdims=True)
    var = jnp.mean(jnp.square(xf - mu), axis=-1, keepdims=True)
    y = (xf - mu) * lax.rsqrt(var + EPS)
    return (y * g.astype(jnp.float32) + b.astype(jnp.float32)).astype(x.dtype)


def causal_multiscale_pool(z):
    B, S, _ = z.shape
    zg = z.reshape(B, S, POOL_GROUPS, POOL_GROUP_DIM).astype(jnp.float32)
    c = jnp.cumsum(zg, axis=1)
    cpad = jnp.pad(c, ((0, 0), (MAX_WINDOW, 0), (0, 0), (0, 0)))
    pos = jnp.arange(S, dtype=jnp.float32)
    outs = []
    for gi, w in enumerate(POOL_WINDOWS):
        window_sum = c[:, :, gi] - cpad[:, MAX_WINDOW - w:MAX_WINDOW - w + S, gi]
        count = jnp.minimum(pos + 1.0, float(w))[None, :, None]
        outs.append(window_sum / count)
    pooled = jnp.stack(outs, axis=2)
    return (pooled - zg).astype(z.dtype)


def causal_depthwise_conv(u, w, b):
    C = u.shape[-1]
    k = w.reshape(CONV_KERNEL, 1, C).astype(u.dtype)
    y = lax.conv_general_dilated(u, k, window_strides=(1,), padding=[(CONV_KERNEL - 1, 0)],
                                 dimension_numbers=('NWC', 'WIO', 'NWC'), feature_group_count=C)
    return y + b.astype(u.dtype)


def peer_token_block(hb, w_q, sub_keys, expert_u, expert_v):
    T = hb.shape[0]
    q = (hb @ w_q).reshape(T, PEER_HEADS, 2, PEER_HALF)
    s = jnp.einsum('thpc,hpkc->thpk', q, sub_keys).astype(jnp.float32)
    vals, idx = lax.top_k(s, PEER_TOPK)
    cand = vals[:, :, 0, :, None] + vals[:, :, 1, None, :]
    cand = cand.reshape(T, PEER_HEADS, PEER_TOPK * PEER_TOPK)
    best, pos = lax.top_k(cand, PEER_TOPK)
    i1 = jnp.take_along_axis(idx[:, :, 0], pos // PEER_TOPK, axis=-1)
    i2 = jnp.take_along_axis(idx[:, :, 1], pos % PEER_TOPK, axis=-1)
    expert = i1 * PEER_N_KEYS + i2
    gate = jax.nn.softmax(best, axis=-1)
    u = jnp.take(expert_u, expert, axis=0)
    v = jnp.take(expert_v, expert, axis=0)
    act = jax.nn.gelu(jnp.einsum('thkd,td->thk', u, hb))
    return jnp.einsum('thk,thkd->td', (gate.astype(hb.dtype) * act), v)


def setup_inputs(seed: int = 0) -> dict:
    key = jax.random.key(seed)
    ks = jax.random.split(key, 24)
    f32 = jnp.float32
    L, D = DEPTH, D_MODEL
    nrm = lambda k, shape, scale: jax.random.normal(k, shape, f32) * scale
    return {
        "x": jax.random.normal(ks[0], (BATCH, SEQ, D), f32),
        "mix_norm_g": 1.0 + nrm(ks[1], (L, D), 0.02),
        "w_in": nrm(ks[2], (L, D, IN_PROJ_DIM), D ** -0.5),
        "pool_w": nrm(ks[3], (L, POOL_GROUPS, POOL_GROUP_DIM, POOL_GROUP_DIM), POOL_GROUP_DIM ** -0.5),
        "pool_scale": 1.0 + nrm(ks[4], (L, POOL_WIDTH), 0.02),
        "w_branch_a": nrm(ks[5], (L, POOL_WIDTH, D), POOL_WIDTH ** -0.5),
        "conv_w": nrm(ks[6], (L, CONV_KERNEL, CONV_WIDTH), CONV_KERNEL ** -0.5),
        "conv_b": nrm(ks[7], (L, CONV_WIDTH), 0.02),
        "conv_ln_g": 1.0 + nrm(ks[8], (L, CONV_WIDTH), 0.02),
        "conv_ln_b": nrm(ks[9], (L, CONV_WIDTH), 0.02),
        "w_branch_b": nrm(ks[10], (L, CONV_WIDTH, D), CONV_WIDTH ** -0.5),
        "gate_b": nrm(ks[11], (L, N_BRANCHES, D), 0.02),
        "w_out": nrm(ks[12], (L, D, D), D ** -0.5),
        "ffn_norm_g": 1.0 + nrm(ks[13], (L, D), 0.02),
        "peer_w_q": nrm(ks[14], (L, D, PEER_HEADS * PEER_KEY_DIM), D ** -0.5),
        "peer_sub_keys": nrm(ks[15], (L, PEER_HEADS, 2, PEER_N_KEYS, PEER_HALF), PEER_HALF ** -0.5),
        "peer_u": nrm(ks[16], (L, PEER_N_EXPERTS, D), D ** -0.5),
        "peer_v": nrm(ks[17], (L, PEER_N_EXPERTS, D), PEER_HEADS ** -0.5),
        "final_norm_g": 1.0 + nrm(ks[18], (D,), 0.02),
    }


def reference(x, mix_norm_g, w_in, pool_w, pool_scale, w_branch_a, conv_w, conv_b, conv_ln_g,
              conv_ln_b, w_branch_b, gate_b, w_out, ffn_norm_g, peer_w_q, peer_sub_keys,
              peer_u, peer_v, final_norm_g):
    B, S, D = x.shape
    T = B * S
    for l in range(DEPTH):
        h = rmsnorm(x, mix_norm_g[l])
        proj = h @ w_in[l]
        za = proj[..., :POOL_WIDTH]
        zb = proj[..., POOL_WIDTH:POOL_WIDTH + 2 * CONV_WIDTH]
        zg = proj[..., POOL_WIDTH + 2 * CONV_WIDTH:].reshape(B, S, N_BRANCHES, D)
        pa = causal_multiscale_pool(za)
        pa = jnp.einsum('bsgc,gcd->bsgd', pa, pool_w[l]).reshape(B, S, POOL_WIDTH) * pool_scale[l]
        ya = pa @ w_branch_a[l]
        ub = jax.nn.glu(zb, axis=-1)
        ub = causal_depthwise_conv(ub, conv_w[l], conv_b[l])
        ub = jax.nn.swish(layernorm(ub, conv_ln_g[l], conv_ln_b[l]))
        yb = ub @ w_branch_b[l]
        gates = jax.nn.sigmoid(zg + gate_b[l])
        merged = gates[:, :, 0] * ya + gates[:, :, 1] * yb
        x = x + merged @ w_out[l]
        h = rmsnorm(x, ffn_norm_g[l])
        hb = h.reshape(T // PEER_TOKEN_BLOCK, PEER_TOKEN_BLOCK, D)
        wq_l, sk_l, u_l, v_l = peer_w_q[l], peer_sub_keys[l], peer_u[l], peer_v[l]
        y = lax.map(lambda blk: peer_token_block(blk, wq_l, sk_l, u_l, v_l), hb)
        x = x + y.reshape(B, S, D)
    return rmsnorm(x, final_norm_g)
```

```python
import functools

import jax
import jax.numpy as jnp
from jax import lax
from jax.experimental import pallas as pl
from jax.experimental.pallas import tpu as pltpu

D_MODEL = 1024
POOL_WIDTH = 512
POOL_GROUPS = 4
POOL_GROUP_DIM = 128
POOL_WINDOWS = (2, 4, 8, 16)
CONV_WIDTH = 512
CONV_KERNEL = 31
PEER_HEADS = 8
PEER_N_KEYS = 128
PEER_TOPK = 16
PEER_HALF = 128
PEER_SEL = PEER_HEADS * PEER_TOPK
EPS = 1e-6

LANES = 128
SUBLANES = 8
ROW_TILES = D_MODEL // LANES
PACKED_TILES = ROW_TILES // 2
HALO = 32
VMEM_LIMIT = 56 * 1024 * 1024

MIX_ROWS = 256
CONV_CHUNK = 32
ROUTE_ROWS = 128
GATHER_ROWS = 256

_NT = (((1,), (1,)), ((), ()))


def _rms(x, g):
    return x * lax.rsqrt(jnp.mean(x * x, axis=-1, keepdims=True) + EPS) * g


def _bdot(a, b):
    return jnp.dot(a.astype(jnp.bfloat16), b, preferred_element_type=jnp.float32)


def _mixer_kernel(x_ref, g1_ref, win_ref, poolw_ref, pscale_ref, wa_ref, convw_ref, convb_ref,
                  lng_ref, lnb_ref, wb_ref, gateb_ref, wout_ref, g2_ref,
                  x1_ref, h_ref, zbuf, ubuf, ybuf):
    s = pl.program_id(1)
    rows = x_ref.shape[0]

    @pl.when(s == 0)
    def _():
        zbuf[0:HALO, :] = jnp.zeros((HALO, POOL_WIDTH), jnp.float32)
        ubuf[0:HALO, :] = jnp.zeros((HALO, CONV_WIDTH), jnp.float32)

    x = x_ref[...]
    proj = _bdot(_rms(x, g1_ref[...]), win_ref[...])
    za = proj[:, :POOL_WIDTH]
    zb = proj[:, POOL_WIDTH:POOL_WIDTH + 2 * CONV_WIDTH]
    zg = proj[:, POOL_WIDTH + 2 * CONV_WIDTH:]

    zbuf[HALO:HALO + rows, :] = za
    pos = (s * rows + lax.broadcasted_iota(jnp.int32, (rows, 1), 0)).astype(jnp.float32)
    pa = []
    for gi, w in enumerate(POOL_WINDOWS):
        cols = slice(gi * POOL_GROUP_DIM, (gi + 1) * POOL_GROUP_DIM)
        zcur = za[:, cols]
        acc = zcur
        for i in range(1, w):
            acc = acc + zbuf[HALO - i:HALO - i + rows, cols]
        pooled = acc / jnp.minimum(pos + 1.0, float(w))
        pa.append(_bdot(pooled - zcur, poolw_ref[gi]))
    pa = jnp.concatenate(pa, axis=1) * pscale_ref[...]
    ya = _bdot(pa, wa_ref[...])

    ubuf[HALO:HALO + rows, :] = zb[:, :CONV_WIDTH] * jax.nn.sigmoid(zb[:, CONV_WIDTH:])
    for c in range(rows // CONV_CHUNK):
        acc = jnp.broadcast_to(convb_ref[...], (CONV_CHUNK, CONV_WIDTH))
        for k in range(CONV_KERNEL):
            start = c * CONV_CHUNK + HALO - (CONV_KERNEL - 1) + k
            acc = acc + convw_ref[k:k + 1, :] * ubuf[start:start + CONV_CHUNK, :]
        ybuf[c * CONV_CHUNK:(c + 1) * CONV_CHUNK, :] = acc
    y = ybuf[...]
    mu = jnp.mean(y, axis=-1, keepdims=True)
    yc = y - mu
    var = jnp.mean(yc * yc, axis=-1, keepdims=True)
    yn = yc * lax.rsqrt(var + EPS) * lng_ref[...] + lnb_ref[...]
    yb = _bdot(yn * jax.nn.sigmoid(yn), wb_ref[...])

    gates = jax.nn.sigmoid(zg + gateb_ref[...])
    merged = gates[:, :D_MODEL] * ya + gates[:, D_MODEL:] * yb
    x1 = x + _bdot(merged, wout_ref[...])
    x1_ref[...] = x1
    h_ref[...] = _rms(x1, g2_ref[...])

    zbuf[0:HALO, :] = zbuf[rows:rows + HALO, :]
    ubuf[0:HALO, :] = ubuf[rows:rows + HALO, :]


def _mixer(x, g1, w_in, pool_w, pool_scale, w_a, conv_w, conv_b, ln_g, ln_b, w_b, gate_b, w_out, g2):
    B, S, D = x.shape
    rows = min(MIX_ROWS, S)
    const = lambda shape: pl.BlockSpec(shape, lambda b, s: (0,) * len(shape))
    tok = pl.BlockSpec((None, rows, D), lambda b, s: (b, s, 0))
    return pl.pallas_call(
        _mixer_kernel,
        grid=(B, S // rows),
        in_specs=[tok, const((1, D)), const(w_in.shape), const(pool_w.shape), const((1, POOL_WIDTH)),
                  const(w_a.shape), const(conv_w.shape), const((1, CONV_WIDTH)), const((1, CONV_WIDTH)),
                  const((1, CONV_WIDTH)), const(w_b.shape), const((1, 2 * D)), const(w_out.shape),
                  const((1, D))],
        out_specs=[tok, tok],
        out_shape=[jax.ShapeDtypeStruct((B, S, D), jnp.float32)] * 2,
        scratch_shapes=[pltpu.VMEM((HALO + rows, POOL_WIDTH), jnp.float32),
                        pltpu.VMEM((HALO + rows, CONV_WIDTH), jnp.float32),
                        pltpu.VMEM((rows, CONV_WIDTH), jnp.float32)],
        compiler_params=pltpu.CompilerParams(dimension_semantics=("arbitrary", "arbitrary"),
                                             vmem_limit_bytes=VMEM_LIMIT),
        name="mixer",
    )(x, g1.reshape(1, D), w_in, pool_w, pool_scale.reshape(1, -1), w_a, conv_w, conv_b.reshape(1, -1),
      ln_g.reshape(1, -1), ln_b.reshape(1, -1), w_b, gate_b.reshape(1, -1), w_out, g2.reshape(1, D))


def _topk_rows(s, k):
    n = s.shape[0]
    iota = lax.broadcasted_iota(jnp.int32, s.shape, 0)
    vals, ids = [], []
    for _ in range(k):
        m = jnp.max(s, axis=0, keepdims=True)
        i = jnp.min(jnp.where(s == m, iota, n), axis=0, keepdims=True)
        vals.append(m)
        ids.append(i)
        s = jnp.where(iota == i, -jnp.inf, s)
    return jnp.concatenate(vals, axis=0), jnp.concatenate(ids, axis=0)


def _take_rows(table, sel):
    out = jnp.zeros_like(sel)
    for c in range(table.shape[0]):
        out = jnp.where(sel == c, table[c:c + 1, :], out)
    return out


def _route_kernel(h_ref, wq_ref, sk_ref, off_ref, gate_ref):
    q = _bdot(h_ref[...], wq_ref[...])
    experts, gates = [], []
    for hd in range(PEER_HEADS):
        vals, ids = [], []
        for p in range(2):
            c0 = (hd * 2 + p) * PEER_HALF
            s_t = lax.dot_general(sk_ref[hd, p], q[:, c0:c0 + PEER_HALF], _NT,
                                  preferred_element_type=jnp.float32)
            v, i = _topk_rows(s_t, PEER_TOPK)
            vals.append(v)
            ids.append(i)
        cand = jnp.concatenate([vals[0][i:i + 1, :] + vals[1] for i in range(PEER_TOPK)], axis=0)
        best, pos = _topk_rows(cand, PEER_TOPK)
        i1 = _take_rows(ids[0], pos // PEER_TOPK)
        i2 = _take_rows(ids[1], pos % PEER_TOPK)
        experts.append((i1 * PEER_N_KEYS + i2) * PACKED_TILES)
        e = jnp.exp(best - best[0:1, :])
        gates.append(e / jnp.sum(e, axis=0, keepdims=True))
    off_ref[...] = jnp.concatenate(experts, axis=0).T
    gate_ref[...] = jnp.concatenate(gates, axis=0).T


def _route(h, w_q, sub_keys):
    T, D = h.shape
    rows = min(ROUTE_ROWS, T)
    tok = lambda width: pl.BlockSpec((rows, width), lambda i: (i, 0))
    return pl.pallas_call(
        _route_kernel,
        grid=(T // rows,),
        in_specs=[tok(D), pl.BlockSpec(w_q.shape, lambda i: (0, 0)),
                  pl.BlockSpec(sub_keys.shape, lambda i: (0, 0, 0, 0))],
        out_specs=[tok(PEER_SEL), tok(PEER_SEL)],
        out_shape=[jax.ShapeDtypeStruct((T, PEER_SEL), jnp.int32),
                   jax.ShapeDtypeStruct((T, PEER_SEL), jnp.float32)],
        compiler_params=pltpu.CompilerParams(dimension_semantics=("arbitrary",),
                                             vmem_limit_bytes=VMEM_LIMIT),
        name="route",
    )(h, w_q, sub_keys)


def _pack_table(w):
    bits = lax.bitcast_convert_type(w.astype(jnp.bfloat16), jnp.uint16).astype(jnp.uint32)
    half = D_MODEL // 2
    words = (bits[:, :half] << 16) | bits[:, half:]
    return words.reshape(w.shape[0] * PACKED_TILES, LANES)


def _unpack_row(words):
    hi = lax.bitcast_convert_type(words & jnp.uint32(0xFFFF0000), jnp.float32)
    lo = lax.bitcast_convert_type(words << 16, jnp.float32)
    return hi, lo


def _table_row(tab, offset):
    return tab[pl.ds(pl.multiple_of(offset, PACKED_TILES), PACKED_TILES), :]


def _act_kernel(off_ref, h_ref, gate_ref, tab, w_ref, prod, act):
    rows = h_ref.shape[0]

    def products(t):
        hv = h_ref[t]
        h_hi, h_lo = hv[:PACKED_TILES], hv[PACKED_TILES:]
        offs = off_ref.at[pl.ds(t * PEER_SEL, PEER_SEL)]
        for j in range(PEER_SEL):
            hi, lo = _unpack_row(_table_row(tab, offs[j]))
            prod[j * PACKED_TILES:(j + 1) * PACKED_TILES, :] = hi * h_hi + lo * h_lo

    def reduce(t):
        folded = prod[pl.ds(0, PEER_SEL, stride=PACKED_TILES), :]
        for r in range(1, PACKED_TILES):
            folded = folded + prod[pl.ds(r, PEER_SEL, stride=PACKED_TILES), :]
        act[pl.ds(t, 1), :] = jnp.sum(folded.T, axis=0, keepdims=True)

    prod[...] = jnp.zeros(prod.shape, jnp.float32)

    def token(t, carry):
        reduce(jnp.maximum(t - 1, 0))
        products(t)
        return carry

    lax.fori_loop(0, rows, token, 0)
    reduce(rows - 1)
    w_ref[...] = gate_ref[...] * jax.nn.gelu(act[...])


def _combine_kernel(off_ref, w_ref, x_ref, g_ref, tab, o_ref):
    rows = x_ref.shape[0]

    def token(t, carry):
        offs = off_ref.at[pl.ds(t * PEER_SEL, PEER_SEL)]
        ws = w_ref.at[pl.ds(t * PEER_SEL, PEER_SEL)]
        acc_hi = jnp.zeros((PACKED_TILES, LANES), jnp.float32)
        acc_lo = jnp.zeros((PACKED_TILES, LANES), jnp.float32)
        for j in range(PEER_SEL):
            hi, lo = _unpack_row(_table_row(tab, offs[j]))
            w = ws[j]
            acc_hi = acc_hi + w * hi
            acc_lo = acc_lo + w * lo
        o_ref[t] = x_ref[t] + jnp.concatenate([acc_hi, acc_lo], axis=0)
        return carry

    lax.fori_loop(0, rows, token, 0)
    x = o_ref[...]
    ms = jnp.sum(jnp.sum(x * x, axis=2, keepdims=True), axis=1, keepdims=True) / D_MODEL
    o_ref[...] = x * lax.rsqrt(ms + EPS) * g_ref[...]


def _gather_specs(rows, tab):
    flat = pl.BlockSpec((rows * PEER_SEL,), lambda i: (i,), memory_space=pltpu.SMEM)
    tiles = pl.BlockSpec((rows, ROW_TILES, LANES), lambda i: (i, 0, 0))
    sel = pl.BlockSpec((rows, PEER_SEL), lambda i: (i, 0))
    table = pl.BlockSpec(tab.shape, lambda i: (0, 0), pipeline_mode=pl.Buffered(1))
    return flat, tiles, sel, table


def _act(off_flat, h3, gate, tab):
    T = h3.shape[0]
    rows = min(GATHER_ROWS, T)
    flat, tiles, sel, table = _gather_specs(rows, tab)
    prod = pltpu.VMEM((PEER_SEL * PACKED_TILES, LANES), jnp.float32)
    return pl.pallas_call(
        _act_kernel,
        grid=(T // rows,),
        in_specs=[flat, tiles, sel, table],
        out_specs=sel,
        out_shape=jax.ShapeDtypeStruct((T, PEER_SEL), jnp.float32),
        scratch_shapes=[prod, pltpu.VMEM((rows, PEER_SEL), jnp.float32)],
        compiler_params=pltpu.CompilerParams(dimension_semantics=("arbitrary",),
                                             vmem_limit_bytes=VMEM_LIMIT),
        name="act",
    )(off_flat, h3, gate, tab)


def _combine(off_flat, w_flat, x3, g3, tab):
    T = x3.shape[0]
    rows = min(GATHER_ROWS, T)
    flat, tiles, _, table = _gather_specs(rows, tab)
    return pl.pallas_call(
        _combine_kernel,
        grid=(T // rows,),
        in_specs=[flat, flat, tiles, pl.BlockSpec((1, ROW_TILES, LANES), lambda i: (0, 0, 0)), table],
        out_specs=tiles,
        out_shape=jax.ShapeDtypeStruct(x3.shape, jnp.float32),
        compiler_params=pltpu.CompilerParams(dimension_semantics=("arbitrary",),
                                             vmem_limit_bytes=VMEM_LIMIT),
        name="combine",
    )(off_flat, w_flat, x3, g3, tab)


def kernel(x, mix_norm_g, w_in, pool_w, pool_scale, w_branch_a, conv_w, conv_b, conv_ln_g, conv_ln_b, w_branch_b, gate_b, w_out, ffn_norm_g, peer_w_q, peer_sub_keys, peer_u, peer_v, final_norm_g):
    B, S, D = x.shape
    T = B * S
    assert mix_norm_g.shape[0] == 1, "the combine kernel fuses the final RMSNorm, so exactly one layer"
    bf = lambda w: w[0].astype(jnp.bfloat16)
    x1, h = _mixer(x, mix_norm_g[0], bf(w_in), bf(pool_w), pool_scale[0], bf(w_branch_a), conv_w[0], conv_b[0],
                   conv_ln_g[0], conv_ln_b[0], bf(w_branch_b), gate_b[0], bf(w_out), ffn_norm_g[0])
    h = h.reshape(T, D)
    off, gate = _route(h, bf(peer_w_q), peer_sub_keys[0])
    off_flat = off.reshape(T * PEER_SEL)
    w = _act(off_flat, h.reshape(T, ROW_TILES, LANES), gate, _pack_table(peer_u[0]))
    out = _combine(off_flat, w.reshape(T * PEER_SEL), x1.reshape(T, ROW_TILES, LANES),
                   final_norm_g.reshape(1, ROW_TILES, LANES), _pack_table(peer_v[0]))
    return out.reshape(B, S, D)
```

```python
import functools

import jax
import jax.numpy as jnp
from jax import lax
from jax.experimental import pallas as pl
from jax.experimental.pallas import tpu as pltpu

D_MODEL = 1024
POOL_WIDTH = 512
POOL_GROUPS = 4
POOL_GROUP_DIM = 128
POOL_WINDOWS = (2, 4, 8, 16)
CONV_WIDTH = 512
CONV_KERNEL = 31
PEER_HEADS = 8
PEER_N_KEYS = 128
PEER_TOPK = 16
PEER_HALF = 128
PEER_SEL = PEER_HEADS * PEER_TOPK
EPS = 1e-6

LANES = 128
SUBLANES = 8
ROW_TILES = D_MODEL // LANES
PACKED_TILES = ROW_TILES // 2
HALO = 32
VMEM_LIMIT = 56 * 1024 * 1024

MIX_ROWS = 256
CONV_CHUNK = 32
ROUTE_ROWS = 128
GATHER_ROWS = 256
ACT_GROUP = 2
COMBINE_CHAINS = 4

_NT = (((1,), (1,)), ((), ()))


def _rms(x, g):
    return x * lax.rsqrt(jnp.mean(x * x, axis=-1, keepdims=True) + EPS) * g


def _bdot(a, b):
    return jnp.dot(a.astype(jnp.bfloat16), b, preferred_element_type=jnp.float32)


def _mixer_kernel(x_ref, g1_ref, win_ref, poolw_ref, pscale_ref, wa_ref, convw_ref, convb_ref,
                  lng_ref, lnb_ref, wb_ref, gateb_ref, wout_ref, g2_ref,
                  x1_ref, h_ref, zbuf, ubuf, ybuf):
    s = pl.program_id(1)
    rows = x_ref.shape[0]

    @pl.when(s == 0)
    def _():
        zbuf[0:HALO, :] = jnp.zeros((HALO, POOL_WIDTH), jnp.float32)
        ubuf[0:HALO, :] = jnp.zeros((HALO, CONV_WIDTH), jnp.float32)

    x = x_ref[...]
    proj = _bdot(_rms(x, g1_ref[...]), win_ref[...])
    za = proj[:, :POOL_WIDTH]
    zb = proj[:, POOL_WIDTH:POOL_WIDTH + 2 * CONV_WIDTH]
    zg = proj[:, POOL_WIDTH + 2 * CONV_WIDTH:]

    zbuf[HALO:HALO + rows, :] = za
    pos = (s * rows + lax.broadcasted_iota(jnp.int32, (rows, 1), 0)).astype(jnp.float32)
    pa = []
    for gi, w in enumerate(POOL_WINDOWS):
        cols = slice(gi * POOL_GROUP_DIM, (gi + 1) * POOL_GROUP_DIM)
        zcur = za[:, cols]
        acc = zcur
        for i in range(1, w):
            acc = acc + zbuf[HALO - i:HALO - i + rows, cols]
        pooled = acc / jnp.minimum(pos + 1.0, float(w))
        pa.append(_bdot(pooled - zcur, poolw_ref[gi]))
    pa = jnp.concatenate(pa, axis=1) * pscale_ref[...]
    ya = _bdot(pa, wa_ref[...])

    ubuf[HALO:HALO + rows, :] = zb[:, :CONV_WIDTH] * jax.nn.sigmoid(zb[:, CONV_WIDTH:])
    for c in range(rows // CONV_CHUNK):
        acc = jnp.broadcast_to(convb_ref[...], (CONV_CHUNK, CONV_WIDTH))
        for k in range(CONV_KERNEL):
            start = c * CONV_CHUNK + HALO - (CONV_KERNEL - 1) + k
            acc = acc + convw_ref[k:k + 1, :] * ubuf[start:start + CONV_CHUNK, :]
        ybuf[c * CONV_CHUNK:(c + 1) * CONV_CHUNK, :] = acc
    y = ybuf[...]
    mu = jnp.mean(y, axis=-1, keepdims=True)
    yc = y - mu
    var = jnp.mean(yc * yc, axis=-1, keepdims=True)
    yn = yc * lax.rsqrt(var + EPS) * lng_ref[...] + lnb_ref[...]
    yb = _bdot(yn * jax.nn.sigmoid(yn), wb_ref[...])

    gates = jax.nn.sigmoid(zg + gateb_ref[...])
    merged = gates[:, :D_MODEL] * ya + gates[:, D_MODEL:] * yb
    x1 = x + _bdot(merged, wout_ref[...])
    x1_ref[...] = x1
    h_ref[...] = _rms(x1, g2_ref[...])

    zbuf[0:HALO, :] = zbuf[rows:rows + HALO, :]
    ubuf[0:HALO, :] = ubuf[rows:rows + HALO, :]


def _mixer(x, g1, w_in, pool_w, pool_scale, w_a, conv_w, conv_b, ln_g, ln_b, w_b, gate_b, w_out, g2):
    B, S, D = x.shape
    rows = min(MIX_ROWS, S)
    const = lambda shape: pl.BlockSpec(shape, lambda b, s: (0,) * len(shape))
    tok = pl.BlockSpec((None, rows, D), lambda b, s: (b, s, 0))
    return pl.pallas_call(
        _mixer_kernel,
        grid=(B, S // rows),
        in_specs=[tok, const((1, D)), const(w_in.shape), const(pool_w.shape), const((1, POOL_WIDTH)),
                  const(w_a.shape), const(conv_w.shape), const((1, CONV_WIDTH)), const((1, CONV_WIDTH)),
                  const((1, CONV_WIDTH)), const(w_b.shape), const((1, 2 * D)), const(w_out.shape),
                  const((1, D))],
        out_specs=[tok, tok],
        out_shape=[jax.ShapeDtypeStruct((B, S, D), jnp.float32)] * 2,
        scratch_shapes=[pltpu.VMEM((HALO + rows, POOL_WIDTH), jnp.float32),
                        pltpu.VMEM((HALO + rows, CONV_WIDTH), jnp.float32),
                        pltpu.VMEM((rows, CONV_WIDTH), jnp.float32)],
        compiler_params=pltpu.CompilerParams(dimension_semantics=("arbitrary", "arbitrary"),
                                             vmem_limit_bytes=VMEM_LIMIT),
        name="mixer",
    )(x, g1.reshape(1, D), w_in, pool_w, pool_scale.reshape(1, -1), w_a, conv_w, conv_b.reshape(1, -1),
      ln_g.reshape(1, -1), ln_b.reshape(1, -1), w_b, gate_b.reshape(1, -1), w_out, g2.reshape(1, D))


def _topk_rows(s, k, payload=None):
    n = s.shape[0]
    iota = lax.broadcasted_iota(jnp.int32, s.shape, 0)
    vals, picks = [], []
    for _ in range(k):
        m = jnp.max(s, axis=0, keepdims=True)
        i = jnp.min(jnp.where(s == m, iota, n), axis=0, keepdims=True)
        hit = iota == i
        vals.append(m)
        picks.append(i if payload is None else jnp.max(jnp.where(hit, payload, -1), axis=0, keepdims=True))
        s = jnp.where(hit, -jnp.inf, s)
    return jnp.concatenate(vals, axis=0), jnp.concatenate(picks, axis=0)


def _pair_candidates(va, vb, ia, ib):
    tokens = va.shape[1]
    row = lax.broadcasted_iota(jnp.int32, (SUBLANES, tokens), 0)
    sums, experts = [], []
    for i in range(PEER_TOPK // 2):
        n = PEER_TOPK // (i + 1)
        width = pl.cdiv(n, SUBLANES) * SUBLANES
        s = va[i:i + 1, :] + vb[:width, :]
        sums.append(s if n == width else jnp.where(row < n, s, -jnp.inf))
        experts.append(ia[i:i + 1, :] * PEER_N_KEYS + ib[:width, :])
    sums.append(va[PEER_TOPK // 2:, :] + vb[0:1, :])
    experts.append(ia[PEER_TOPK // 2:, :] * PEER_N_KEYS + ib[0:1, :])
    return jnp.concatenate(sums, axis=0), jnp.concatenate(experts, axis=0)


def _route_kernel(h_ref, wq_ref, sk_ref, off_ref, gate_ref):
    q = _bdot(h_ref[...], wq_ref[...])
    experts, gates = [], []
    for hd in range(PEER_HEADS):
        vals, ids = [], []
        for p in range(2):
            c0 = (hd * 2 + p) * PEER_HALF
            s_t = lax.dot_general(sk_ref[hd, p], q[:, c0:c0 + PEER_HALF], _NT,
                                  preferred_element_type=jnp.float32)
            v, i = _topk_rows(s_t, PEER_TOPK)
            vals.append(v)
            ids.append(i)
        cand, cand_expert = _pair_candidates(vals[0], vals[1], ids[0], ids[1])
        best, expert = _topk_rows(cand, PEER_TOPK, cand_expert)
        experts.append(expert * PACKED_TILES)
        e = jnp.exp(best - best[0:1, :])
        gates.append(e / jnp.sum(e, axis=0, keepdims=True))
    off_ref[...] = jnp.concatenate(experts, axis=0).T
    gate_ref[...] = jnp.concatenate(gates, axis=0).T


def _route(h, w_q, sub_keys):
    T, D = h.shape
    rows = min(ROUTE_ROWS, T)
    tok = lambda width: pl.BlockSpec((rows, width), lambda i: (i, 0))
    return pl.pallas_call(
        _route_kernel,
        grid=(T // rows,),
        in_specs=[tok(D), pl.BlockSpec(w_q.shape, lambda i: (0, 0)),
                  pl.BlockSpec(sub_keys.shape, lambda i: (0, 0, 0, 0))],
        out_specs=[tok(PEER_SEL), tok(PEER_SEL)],
        out_shape=[jax.ShapeDtypeStruct((T, PEER_SEL), jnp.int32),
                   jax.ShapeDtypeStruct((T, PEER_SEL), jnp.float32)],
        compiler_params=pltpu.CompilerParams(dimension_semantics=("arbitrary",),
                                             vmem_limit_bytes=VMEM_LIMIT),
        name="route",
    )(h, w_q, sub_keys)


def _pack_table(w):
    bits = lax.bitcast_convert_type(w.astype(jnp.bfloat16), jnp.uint16).astype(jnp.uint32)
    half = D_MODEL // 2
    words = (bits[:, :half] << 16) | bits[:, half:]
    return words.reshape(w.shape[0] * PACKED_TILES, LANES)


def _table_row(tab, offset):
    words = tab[pl.ds(pl.multiple_of(offset, PACKED_TILES), PACKED_TILES), :]
    hi = lax.bitcast_convert_type(words & jnp.uint32(0xFFFF0000), jnp.float32)
    lo = lax.bitcast_convert_type(words << 16, jnp.float32)
    return hi, lo


def _act_kernel(off_ref, h_ref, gate_ref, tab, w_ref, prod, act):
    rows = h_ref.shape[0]

    def products(t0):
        hs = [h_ref[t0 + g] for g in range(ACT_GROUP)]
        offs = [off_ref.at[pl.ds((t0 + g) * PEER_SEL, PEER_SEL)] for g in range(ACT_GROUP)]
        for j in range(PEER_SEL):
            for g in range(ACT_GROUP):
                hi, lo = _table_row(tab, offs[g][j])
                prod[g, j * PACKED_TILES:(j + 1) * PACKED_TILES, :] = (
                    hi * hs[g][:PACKED_TILES] + lo * hs[g][PACKED_TILES:])

    def reduce(t0):
        for g in range(ACT_GROUP):
            folded = prod[g, pl.ds(0, PEER_SEL, stride=PACKED_TILES), :]
            for r in range(1, PACKED_TILES):
                folded = folded + prod[g, pl.ds(r, PEER_SEL, stride=PACKED_TILES), :]
            act[pl.ds(t0 + g, 1), :] = jnp.sum(folded.T, axis=0, keepdims=True)

    prod[...] = jnp.zeros(prod.shape, jnp.float32)

    def group(i, carry):
        reduce(jnp.maximum(i - 1, 0) * ACT_GROUP)
        products(i * ACT_GROUP)
        return carry

    lax.fori_loop(0, rows // ACT_GROUP, group, 0)
    reduce(rows - ACT_GROUP)
    w_ref[...] = gate_ref[...] * jax.nn.gelu(act[...])


def _combine_kernel(off_ref, w_ref, x_ref, g_ref, tab, o_ref, splat):
    rows = x_ref.shape[0]

    def spread(t):
        return jnp.broadcast_to(w_ref[pl.ds(t, 1), :], (LANES, PEER_SEL)).T

    splat[...] = spread(0)

    def token(t, carry):
        nxt = spread(jnp.minimum(t + 1, rows - 1))
        offs = off_ref.at[pl.ds(t * PEER_SEL, PEER_SEL)]
        acc_hi = [jnp.zeros((PACKED_TILES, LANES), jnp.float32) for _ in range(COMBINE_CHAINS)]
        acc_lo = [jnp.zeros((PACKED_TILES, LANES), jnp.float32) for _ in range(COMBINE_CHAINS)]
        for j in range(PEER_SEL):
            hi, lo = _table_row(tab, offs[j])
            w = jnp.broadcast_to(splat[j:j + 1, :], (PACKED_TILES, LANES))
            acc_hi[j % COMBINE_CHAINS] = acc_hi[j % COMBINE_CHAINS] + w * hi
            acc_lo[j % COMBINE_CHAINS] = acc_lo[j % COMBINE_CHAINS] + w * lo
        o_ref[t] = x_ref[t] + jnp.concatenate([sum(acc_hi), sum(acc_lo)], axis=0)
        splat[...] = nxt
        return carry

    lax.fori_loop(0, rows, token, 0)
    x = o_ref[...]
    ms = jnp.sum(jnp.sum(x * x, axis=2, keepdims=True), axis=1, keepdims=True) / D_MODEL
    o_ref[...] = x * lax.rsqrt(ms + EPS) * g_ref[...]


def _gather_specs(rows, tab):
    flat = pl.BlockSpec((rows * PEER_SEL,), lambda i: (i,), memory_space=pltpu.SMEM)
    tiles = pl.BlockSpec((rows, ROW_TILES, LANES), lambda i: (i, 0, 0))
    sel = pl.BlockSpec((rows, PEER_SEL), lambda i: (i, 0))
    table = pl.BlockSpec(tab.shape, lambda i: (0, 0), pipeline_mode=pl.Buffered(1))
    return flat, tiles, sel, table


def _act(off_flat, h3, gate, tab):
    T = h3.shape[0]
    rows = min(GATHER_ROWS, T)
    flat, tiles, sel, table = _gather_specs(rows, tab)
    prod = pltpu.VMEM((ACT_GROUP, PEER_SEL * PACKED_TILES, LANES), jnp.float32)
    return pl.pallas_call(
        _act_kernel,
        grid=(T // rows,),
        in_specs=[flat, tiles, sel, table],
        out_specs=sel,
        out_shape=jax.ShapeDtypeStruct((T, PEER_SEL), jnp.float32),
        scratch_shapes=[prod, pltpu.VMEM((rows, PEER_SEL), jnp.float32)],
        compiler_params=pltpu.CompilerParams(dimension_semantics=("arbitrary",),
                                             vmem_limit_bytes=VMEM_LIMIT),
        name="act",
    )(off_flat, h3, gate, tab)


def _combine(off_flat, w, x3, g3, tab):
    T = x3.shape[0]
    rows = min(GATHER_ROWS, T)
    flat, tiles, sel, table = _gather_specs(rows, tab)
    return pl.pallas_call(
        _combine_kernel,
        grid=(T // rows,),
        in_specs=[flat, sel, tiles, pl.BlockSpec((1, ROW_TILES, LANES), lambda i: (0, 0, 0)), table],
        out_specs=tiles,
        out_shape=jax.ShapeDtypeStruct(x3.shape, jnp.float32),
        scratch_shapes=[pltpu.VMEM((PEER_SEL, LANES), jnp.float32)],
        compiler_params=pltpu.CompilerParams(dimension_semantics=("arbitrary",),
                                             vmem_limit_bytes=VMEM_LIMIT),
        name="combine",
    )(off_flat, w, x3, g3, tab)


def kernel(x, mix_norm_g, w_in, pool_w, pool_scale, w_branch_a, conv_w, conv_b, conv_ln_g, conv_ln_b, w_branch_b, gate_b, w_out, ffn_norm_g, peer_w_q, peer_sub_keys, peer_u, peer_v, final_norm_g):
    B, S, D = x.shape
    T = B * S
    assert mix_norm_g.shape[0] == 1, "the combine kernel fuses the final RMSNorm, so exactly one layer"
    bf = lambda w: w[0].astype(jnp.bfloat16)
    x1, h = _mixer(x, mix_norm_g[0], bf(w_in), bf(pool_w), pool_scale[0], bf(w_branch_a), conv_w[0], conv_b[0],
                   conv_ln_g[0], conv_ln_b[0], bf(w_branch_b), gate_b[0], bf(w_out), ffn_norm_g[0])
    h = h.reshape(T, D)
    off, gate = _route(h, bf(peer_w_q), peer_sub_keys[0])
    off_flat = off.reshape(T * PEER_SEL)
    w = _act(off_flat, h.reshape(T, ROW_TILES, LANES), gate, _pack_table(peer_u[0]))
    out = _combine(off_flat, w, x1.reshape(T, ROW_TILES, LANES),
                   final_norm_g.reshape(1, ROW_TILES, LANES), _pack_table(peer_v[0]))
    return out.reshape(B, S, D)
```

```python
import functools

import jax
import jax.numpy as jnp
from jax import lax
from jax.experimental import pallas as pl
from jax.experimental.pallas import tpu as pltpu

D_MODEL = 1024
POOL_WIDTH = 512
POOL_GROUPS = 4
POOL_GROUP_DIM = 128
POOL_WINDOWS = (2, 4, 8, 16)
CONV_WIDTH = 512
CONV_KERNEL = 31
PEER_HEADS = 8
PEER_N_KEYS = 128
PEER_TOPK = 16
PEER_HALF = 128
PEER_SEL = PEER_HEADS * PEER_TOPK
EPS = 1e-6

LANES = 128
SUBLANES = 8
ROW_TILES = D_MODEL // LANES
PACKED_TILES = ROW_TILES // 2
HALO = 32
VMEM_LIMIT = 56 * 1024 * 1024

MIX_ROWS = 256
CONV_CHUNK = 32
ROUTE_ROWS = 128
GATHER_ROWS = 256
ACT_GROUP = 4
COMBINE_GROUP = 2
COMBINE_CHAINS = 2

_NT = (((1,), (1,)), ((), ()))


def _rms(x, g):
    return x * lax.rsqrt(jnp.mean(x * x, axis=-1, keepdims=True) + EPS) * g


def _bdot(a, b):
    return jnp.dot(a.astype(jnp.bfloat16), b, preferred_element_type=jnp.float32)


def _mixer_kernel(x_ref, g1_ref, win_ref, poolw_ref, pscale_ref, wa_ref, convw_ref, convb_ref,
                  lng_ref, lnb_ref, wb_ref, gateb_ref, wout_ref, g2_ref,
                  x1_ref, h_ref, zbuf, ubuf, ybuf):
    s = pl.program_id(1)
    rows = x_ref.shape[0]

    @pl.when(s == 0)
    def _():
        zbuf[0:HALO, :] = jnp.zeros((HALO, POOL_WIDTH), jnp.float32)
        ubuf[0:HALO, :] = jnp.zeros((HALO, CONV_WIDTH), jnp.float32)

    x = x_ref[...]
    proj = _bdot(_rms(x, g1_ref[...]), win_ref[...])
    za = proj[:, :POOL_WIDTH]
    zb = proj[:, POOL_WIDTH:POOL_WIDTH + 2 * CONV_WIDTH]
    zg = proj[:, POOL_WIDTH + 2 * CONV_WIDTH:]

    zbuf[HALO:HALO + rows, :] = za
    pos = (s * rows + lax.broadcasted_iota(jnp.int32, (rows, 1), 0)).astype(jnp.float32)
    pa = []
    for gi, w in enumerate(POOL_WINDOWS):
        cols = slice(gi * POOL_GROUP_DIM, (gi + 1) * POOL_GROUP_DIM)
        zcur = za[:, cols]
        acc = zcur
        for i in range(1, w):
            acc = acc + zbuf[HALO - i:HALO - i + rows, cols]
        pooled = acc / jnp.minimum(pos + 1.0, float(w))
        pa.append(_bdot(pooled - zcur, poolw_ref[gi]))
    pa = jnp.concatenate(pa, axis=1) * pscale_ref[...]
    ya = _bdot(pa, wa_ref[...])

    ubuf[HALO:HALO + rows, :] = zb[:, :CONV_WIDTH] * jax.nn.sigmoid(zb[:, CONV_WIDTH:])
    for c in range(rows // CONV_CHUNK):
        acc = jnp.broadcast_to(convb_ref[...], (CONV_CHUNK, CONV_WIDTH))
        for k in range(CONV_KERNEL):
            start = c * CONV_CHUNK + HALO - (CONV_KERNEL - 1) + k
            acc = acc + convw_ref[k:k + 1, :] * ubuf[start:start + CONV_CHUNK, :]
        ybuf[c * CONV_CHUNK:(c + 1) * CONV_CHUNK, :] = acc
    y = ybuf[...]
    mu = jnp.mean(y, axis=-1, keepdims=True)
    yc = y - mu
    var = jnp.mean(yc * yc, axis=-1, keepdims=True)
    yn = yc * lax.rsqrt(var + EPS) * lng_ref[...] + lnb_ref[...]
    yb = _bdot(yn * jax.nn.sigmoid(yn), wb_ref[...])

    gates = jax.nn.sigmoid(zg + gateb_ref[...])
    merged = gates[:, :D_MODEL] * ya + gates[:, D_MODEL:] * yb
    x1 = x + _bdot(merged, wout_ref[...])
    x1_ref[...] = x1
    h_ref[...] = _rms(x1, g2_ref[...])

    zbuf[0:HALO, :] = zbuf[rows:rows + HALO, :]
    ubuf[0:HALO, :] = ubuf[rows:rows + HALO, :]


def _mixer(x, g1, w_in, pool_w, pool_scale, w_a, conv_w, conv_b, ln_g, ln_b, w_b, gate_b, w_out, g2):
    B, S, D = x.shape
    rows = min(MIX_ROWS, S)
    const = lambda shape: pl.BlockSpec(shape, lambda b, s: (0,) * len(shape))
    tok = pl.BlockSpec((None, rows, D), lambda b, s: (b, s, 0))
    return pl.pallas_call(
        _mixer_kernel,
        grid=(B, S // rows),
        in_specs=[tok, const((1, D)), const(w_in.shape), const(pool_w.shape), const((1, POOL_WIDTH)),
                  const(w_a.shape), const(conv_w.shape), const((1, CONV_WIDTH)), const((1, CONV_WIDTH)),
                  const((1, CONV_WIDTH)), const(w_b.shape), const((1, 2 * D)), const(w_out.shape),
                  const((1, D))],
        out_specs=[tok, tok],
        out_shape=[jax.ShapeDtypeStruct((B, S, D), jnp.float32)] * 2,
        scratch_shapes=[pltpu.VMEM((HALO + rows, POOL_WIDTH), jnp.float32),
                        pltpu.VMEM((HALO + rows, CONV_WIDTH), jnp.float32),
                        pltpu.VMEM((rows, CONV_WIDTH), jnp.float32)],
        compiler_params=pltpu.CompilerParams(dimension_semantics=("arbitrary", "arbitrary"),
                                             vmem_limit_bytes=VMEM_LIMIT),
        name="mixer",
    )(x, g1.reshape(1, D), w_in, pool_w, pool_scale.reshape(1, -1), w_a, conv_w, conv_b.reshape(1, -1),
      ln_g.reshape(1, -1), ln_b.reshape(1, -1), w_b, gate_b.reshape(1, -1), w_out, g2.reshape(1, D))


def _topk_rows(s, k, payload=None):
    n = s.shape[0]
    iota = lax.broadcasted_iota(jnp.int32, s.shape, 0)
    vals, picks = [], []
    for _ in range(k):
        m = jnp.max(s, axis=0, keepdims=True)
        i = jnp.min(jnp.where(s == m, iota, n), axis=0, keepdims=True)
        hit = iota == i
        vals.append(m)
        picks.append(i if payload is None else jnp.max(jnp.where(hit, payload, -1), axis=0, keepdims=True))
        s = jnp.where(hit, -jnp.inf, s)
    return jnp.concatenate(vals, axis=0), jnp.concatenate(picks, axis=0)


def _pair_candidates(va, vb, ia, ib):
    tokens = va.shape[1]
    row = lax.broadcasted_iota(jnp.int32, (SUBLANES, tokens), 0)
    sums, experts = [], []
    for i in range(PEER_TOPK // 2):
        n = PEER_TOPK // (i + 1)
        width = pl.cdiv(n, SUBLANES) * SUBLANES
        s = va[i:i + 1, :] + vb[:width, :]
        sums.append(s if n == width else jnp.where(row < n, s, -jnp.inf))
        experts.append(ia[i:i + 1, :] * PEER_N_KEYS + ib[:width, :])
    sums.append(va[PEER_TOPK // 2:, :] + vb[0:1, :])
    experts.append(ia[PEER_TOPK // 2:, :] * PEER_N_KEYS + ib[0:1, :])
    return jnp.concatenate(sums, axis=0), jnp.concatenate(experts, axis=0)


def _route_kernel(h_ref, wq_ref, sk_ref, off_ref, gate_ref):
    q = _bdot(h_ref[...], wq_ref[...])
    experts, gates = [], []
    for hd in range(PEER_HEADS):
        vals, ids = [], []
        for p in range(2):
            c0 = (hd * 2 + p) * PEER_HALF
            s_t = lax.dot_general(sk_ref[hd, p], q[:, c0:c0 + PEER_HALF], _NT,
                                  preferred_element_type=jnp.float32)
            v, i = _topk_rows(s_t, PEER_TOPK)
            vals.append(v)
            ids.append(i)
        cand, cand_expert = _pair_candidates(vals[0], vals[1], ids[0], ids[1])
        best, expert = _topk_rows(cand, PEER_TOPK, cand_expert)
        experts.append(expert * PACKED_TILES)
        e = jnp.exp(best - best[0:1, :])
        gates.append(e / jnp.sum(e, axis=0, keepdims=True))
    off_ref[...] = jnp.concatenate(experts, axis=0).T
    gate_ref[...] = jnp.concatenate(gates, axis=0).T


def _route(h, w_q, sub_keys):
    T, D = h.shape
    rows = min(ROUTE_ROWS, T)
    tok = lambda width: pl.BlockSpec((rows, width), lambda i: (i, 0))
    return pl.pallas_call(
        _route_kernel,
        grid=(T // rows,),
        in_specs=[tok(D), pl.BlockSpec(w_q.shape, lambda i: (0, 0)),
                  pl.BlockSpec(sub_keys.shape, lambda i: (0, 0, 0, 0))],
        out_specs=[tok(PEER_SEL), tok(PEER_SEL)],
        out_shape=[jax.ShapeDtypeStruct((T, PEER_SEL), jnp.int32),
                   jax.ShapeDtypeStruct((T, PEER_SEL), jnp.float32)],
        compiler_params=pltpu.CompilerParams(dimension_semantics=("arbitrary",),
                                             vmem_limit_bytes=VMEM_LIMIT),
        name="route",
    )(h, w_q, sub_keys)


def _pack_table(w):
    bits = lax.bitcast_convert_type(w.astype(jnp.bfloat16), jnp.uint16).astype(jnp.uint32)
    bits = bits.reshape(w.shape[0], PACKED_TILES, 2, LANES)
    words = (bits[:, :, 1, :] << 16) | bits[:, :, 0, :]
    return words.reshape(w.shape[0] * PACKED_TILES, LANES)


def _table_tile(tab, offset):
    words = tab[pl.ds(pl.multiple_of(offset, PACKED_TILES), PACKED_TILES), :]
    return pltpu.bitcast(words, jnp.bfloat16).astype(jnp.float32)


def _act_kernel(off_ref, h_ref, gate_ref, tab, w_ref, prod, act):
    rows = h_ref.shape[0]

    def products(t0):
        hs = [h_ref[t0 + g] for g in range(ACT_GROUP)]
        offs = [off_ref.at[pl.ds((t0 + g) * PEER_SEL, PEER_SEL)] for g in range(ACT_GROUP)]
        for j in range(PEER_SEL):
            for g in range(ACT_GROUP):
                p = _table_tile(tab, offs[g][j]) * hs[g]
                prod[g, j * PACKED_TILES:(j + 1) * PACKED_TILES, :] = p[:PACKED_TILES] + p[PACKED_TILES:]

    def reduce(t0):
        for g in range(ACT_GROUP):
            folded = prod[g, pl.ds(0, PEER_SEL, stride=PACKED_TILES), :]
            for r in range(1, PACKED_TILES):
                folded = folded + prod[g, pl.ds(r, PEER_SEL, stride=PACKED_TILES), :]
            act[pl.ds(t0 + g, 1), :] = jnp.sum(folded.T, axis=0, keepdims=True)

    prod[...] = jnp.zeros(prod.shape, jnp.float32)

    def group(i, carry):
        reduce(jnp.maximum(i - 1, 0) * ACT_GROUP)
        products(i * ACT_GROUP)
        return carry

    lax.fori_loop(0, rows // ACT_GROUP, group, 0)
    reduce(rows - ACT_GROUP)
    w_ref[...] = gate_ref[...] * jax.nn.gelu(act[...])


def _combine_kernel(off_ref, w_ref, x_ref, g_ref, tab, o_ref, splat):
    rows = x_ref.shape[0]

    def spread(t):
        return jnp.broadcast_to(w_ref[pl.ds(t, 1), :], (LANES, PEER_SEL)).T

    for g in range(COMBINE_GROUP):
        splat[g] = spread(g)

    def group(i, carry):
        t0 = i * COMBINE_GROUP
        nxt = [spread(jnp.minimum(t0 + COMBINE_GROUP + g, rows - 1)) for g in range(COMBINE_GROUP)]
        offs = [off_ref.at[pl.ds((t0 + g) * PEER_SEL, PEER_SEL)] for g in range(COMBINE_GROUP)]
        acc = [[x_ref[t0 + g]] + [jnp.zeros((ROW_TILES, LANES), jnp.float32)] * (COMBINE_CHAINS - 1)
               for g in range(COMBINE_GROUP)]
        for j in range(PEER_SEL):
            c = j % COMBINE_CHAINS
            for g in range(COMBINE_GROUP):
                w = jnp.broadcast_to(splat[g, j:j + 1, :], (ROW_TILES, LANES))
                acc[g][c] = acc[g][c] + w * _table_tile(tab, offs[g][j])
        for g in range(COMBINE_GROUP):
            o_ref[t0 + g] = sum(acc[g])
            splat[g] = nxt[g]
        return carry

    lax.fori_loop(0, rows // COMBINE_GROUP, group, 0)
    x = o_ref[...]
    ms = jnp.sum(jnp.sum(x * x, axis=2, keepdims=True), axis=1, keepdims=True) / D_MODEL
    o_ref[...] = x * lax.rsqrt(ms + EPS) * g_ref[...]


def _gather_specs(rows, tab):
    flat = pl.BlockSpec((rows * PEER_SEL,), lambda i: (i,), memory_space=pltpu.SMEM)
    tiles = pl.BlockSpec((rows, ROW_TILES, LANES), lambda i: (i, 0, 0))
    sel = pl.BlockSpec((rows, PEER_SEL), lambda i: (i, 0))
    table = pl.BlockSpec(tab.shape, lambda i: (0, 0), pipeline_mode=pl.Buffered(1))
    return flat, tiles, sel, table


def _act(off_flat, h3, gate, tab):
    T = h3.shape[0]
    rows = min(GATHER_ROWS, T)
    flat, tiles, sel, table = _gather_specs(rows, tab)
    prod = pltpu.VMEM((ACT_GROUP, PEER_SEL * PACKED_TILES, LANES), jnp.float32)
    return pl.pallas_call(
        _act_kernel,
        grid=(T // rows,),
        in_specs=[flat, tiles, sel, table],
        out_specs=sel,
        out_shape=jax.ShapeDtypeStruct((T, PEER_SEL), jnp.float32),
        scratch_shapes=[prod, pltpu.VMEM((rows, PEER_SEL), jnp.float32)],
        compiler_params=pltpu.CompilerParams(dimension_semantics=("arbitrary",),
                                             vmem_limit_bytes=VMEM_LIMIT),
        name="act",
    )(off_flat, h3, gate, tab)


def _combine(off_flat, w, x3, g3, tab):
    T = x3.shape[0]
    rows = min(GATHER_ROWS, T)
    flat, tiles, sel, table = _gather_specs(rows, tab)
    return pl.pallas_call(
        _combine_kernel,
        grid=(T // rows,),
        in_specs=[flat, sel, tiles, pl.BlockSpec((1, ROW_TILES, LANES), lambda i: (0, 0, 0)), table],
        out_specs=tiles,
        out_shape=jax.ShapeDtypeStruct(x3.shape, jnp.float32),
        scratch_shapes=[pltpu.VMEM((COMBINE_GROUP, PEER_SEL, LANES), jnp.float32)],
        compiler_params=pltpu.CompilerParams(dimension_semantics=("arbitrary",),
                                             vmem_limit_bytes=VMEM_LIMIT),
        name="combine",
    )(off_flat, w, x3, g3, tab)


def kernel(x, mix_norm_g, w_in, pool_w, pool_scale, w_branch_a, conv_w, conv_b, conv_ln_g, conv_ln_b, w_branch_b, gate_b, w_out, ffn_norm_g, peer_w_q, peer_sub_keys, peer_u, peer_v, final_norm_g):
    B, S, D = x.shape
    T = B * S
    assert mix_norm_g.shape[0] == 1, "the combine kernel fuses the final RMSNorm, so exactly one layer"
    bf = lambda w: w[0].astype(jnp.bfloat16)
    x1, h = _mixer(x, mix_norm_g[0], bf(w_in), bf(pool_w), pool_scale[0], bf(w_branch_a), conv_w[0], conv_b[0],
                   conv_ln_g[0], conv_ln_b[0], bf(w_branch_b), gate_b[0], bf(w_out), ffn_norm_g[0])
    h = h.reshape(T, D)
    off, gate = _route(h, bf(peer_w_q), peer_sub_keys[0])
    off_flat = off.reshape(T * PEER_SEL)
    w = _act(off_flat, h.reshape(T, ROW_TILES, LANES), gate, _pack_table(peer_u[0]))
    out = _combine(off_flat, w, x1.reshape(T, ROW_TILES, LANES),
                   final_norm_g.reshape(1, ROW_TILES, LANES), _pack_table(peer_v[0]))
    return out.reshape(B, S, D)
```

```python
import functools

import jax
import jax.numpy as jnp
from jax import lax
from jax.experimental import pallas as pl
from jax.experimental.pallas import tpu as pltpu

D_MODEL = 1024
POOL_WIDTH = 512
POOL_GROUPS = 4
POOL_GROUP_DIM = 128
POOL_WINDOWS = (2, 4, 8, 16)
CONV_WIDTH = 512
CONV_KERNEL = 31
PEER_HEADS = 8
PEER_N_KEYS = 128
PEER_TOPK = 16
PEER_HALF = 128
PEER_SEL = PEER_HEADS * PEER_TOPK
EPS = 1e-6

LANES = 128
SUBLANES = 8
ROW_TILES = D_MODEL // LANES
PACKED_TILES = ROW_TILES // 2
HALO = 32
VMEM_LIMIT = 56 * 1024 * 1024

MIX_ROWS = 256
CONV_CHUNK = 32
ROUTE_ROWS = 128
GATHER_ROWS = 256
ACT_GROUP = 4
OFFSET_WINDOW = 8
PACK_ROWS = 512
COMBINE_GROUP = 2
COMBINE_CHAINS = 2

_NT = (((1,), (1,)), ((), ()))


def _rms(x, g):
    return x * lax.rsqrt(jnp.mean(x * x, axis=-1, keepdims=True) + EPS) * g


def _bdot(a, b):
    return jnp.dot(a.astype(jnp.bfloat16), b, preferred_element_type=jnp.float32)


def _mixer_kernel(x_ref, g1_ref, win_ref, poolw_ref, pscale_ref, wa_ref, convw_ref, convb_ref,
                  lng_ref, lnb_ref, wb_ref, gateb_ref, wout_ref, g2_ref,
                  x1_ref, h_ref, zbuf, ubuf, ybuf, ushift):
    s = pl.program_id(1)
    rows = x_ref.shape[0]

    @pl.when(s == 0)
    def _():
        zbuf[0:HALO, :] = jnp.zeros((HALO, POOL_WIDTH), jnp.float32)
        ubuf[0:HALO, :] = jnp.zeros((HALO, CONV_WIDTH), jnp.float32)

    x = x_ref[...]
    proj = _bdot(_rms(x, g1_ref[...]), win_ref[...])
    za = proj[:, :POOL_WIDTH]
    zb = proj[:, POOL_WIDTH:POOL_WIDTH + 2 * CONV_WIDTH]
    zg = proj[:, POOL_WIDTH + 2 * CONV_WIDTH:]

    zbuf[HALO:HALO + rows, :] = za
    pos = (s * rows + lax.broadcasted_iota(jnp.int32, (rows, 1), 0)).astype(jnp.float32)
    pa = []
    for gi, w in enumerate(POOL_WINDOWS):
        cols = slice(gi * POOL_GROUP_DIM, (gi + 1) * POOL_GROUP_DIM)
        zcur = za[:, cols]
        acc = zcur
        for i in range(1, w):
            acc = acc + zbuf[HALO - i:HALO - i + rows, cols]
        pooled = acc / jnp.minimum(pos + 1.0, float(w))
        pa.append(_bdot(pooled - zcur, poolw_ref[gi]))
    pa = jnp.concatenate(pa, axis=1) * pscale_ref[...]
    ya = _bdot(pa, wa_ref[...])

    ubuf[HALO:HALO + rows, :] = zb[:, :CONV_WIDTH] * jax.nn.sigmoid(zb[:, CONV_WIDTH:])
    span = rows + HALO - SUBLANES
    for r in range(1, SUBLANES):
        ushift[r - 1, 0:span, :] = ubuf[r:r + span, :]
    for c in range(rows // CONV_CHUNK):
        acc = jnp.broadcast_to(convb_ref[...], (CONV_CHUNK, CONV_WIDTH))
        for k in range(CONV_KERNEL):
            start = c * CONV_CHUNK + HALO - (CONV_KERNEL - 1) + k
            r, base = start % SUBLANES, start - start % SUBLANES
            taps = ubuf if r == 0 else ushift.at[r - 1]
            acc = acc + convw_ref[k:k + 1, :] * taps[base:base + CONV_CHUNK, :]
        ybuf[c * CONV_CHUNK:(c + 1) * CONV_CHUNK, :] = acc
    y = ybuf[...]
    mu = jnp.mean(y, axis=-1, keepdims=True)
    yc = y - mu
    var = jnp.mean(yc * yc, axis=-1, keepdims=True)
    yn = yc * lax.rsqrt(var + EPS) * lng_ref[...] + lnb_ref[...]
    yb = _bdot(yn * jax.nn.sigmoid(yn), wb_ref[...])

    gates = jax.nn.sigmoid(zg + gateb_ref[...])
    merged = gates[:, :D_MODEL] * ya + gates[:, D_MODEL:] * yb
    x1 = x + _bdot(merged, wout_ref[...])
    x1_ref[...] = x1
    h_ref[...] = _rms(x1, g2_ref[...])

    zbuf[0:HALO, :] = zbuf[rows:rows + HALO, :]
    ubuf[0:HALO, :] = ubuf[rows:rows + HALO, :]


def _mixer(x, g1, w_in, pool_w, pool_scale, w_a, conv_w, conv_b, ln_g, ln_b, w_b, gate_b, w_out, g2):
    B, S, D = x.shape
    rows = min(MIX_ROWS, S)
    const = lambda shape: pl.BlockSpec(shape, lambda b, s: (0,) * len(shape))
    tok = pl.BlockSpec((None, rows, D), lambda b, s: (b, s, 0))
    return pl.pallas_call(
        _mixer_kernel,
        grid=(B, S // rows),
        in_specs=[tok, const((1, D)), const(w_in.shape), const(pool_w.shape), const((1, POOL_WIDTH)),
                  const(w_a.shape), const(conv_w.shape), const((1, CONV_WIDTH)), const((1, CONV_WIDTH)),
                  const((1, CONV_WIDTH)), const(w_b.shape), const((1, 2 * D)), const(w_out.shape),
                  const((1, D))],
        out_specs=[tok, tok],
        out_shape=[jax.ShapeDtypeStruct((B, S, D), jnp.float32)] * 2,
        scratch_shapes=[pltpu.VMEM((HALO + rows, POOL_WIDTH), jnp.float32),
                        pltpu.VMEM((HALO + rows, CONV_WIDTH), jnp.float32),
                        pltpu.VMEM((rows, CONV_WIDTH), jnp.float32),
                        pltpu.VMEM((SUBLANES - 1, HALO + rows, CONV_WIDTH), jnp.float32)],
        compiler_params=pltpu.CompilerParams(dimension_semantics=("arbitrary", "arbitrary"),
                                             vmem_limit_bytes=VMEM_LIMIT),
        name="mixer",
    )(x, g1.reshape(1, D), w_in, pool_w, pool_scale.reshape(1, -1), w_a, conv_w, conv_b.reshape(1, -1),
      ln_g.reshape(1, -1), ln_b.reshape(1, -1), w_b, gate_b.reshape(1, -1), w_out, g2.reshape(1, D))


def _topk_rows(s, k, payload=None):
    n = s.shape[0]
    iota = lax.broadcasted_iota(jnp.int32, s.shape, 0)
    vals, picks = [], []
    for _ in range(k):
        m = jnp.max(s, axis=0, keepdims=True)
        i = jnp.min(jnp.where(s == m, iota, n), axis=0, keepdims=True)
        hit = iota == i
        vals.append(m)
        picks.append(i if payload is None else jnp.max(jnp.where(hit, payload, -1), axis=0, keepdims=True))
        s = jnp.where(hit, -jnp.inf, s)
    return jnp.concatenate(vals, axis=0), jnp.concatenate(picks, axis=0)


def _pair_candidates(va, vb, ia, ib):
    tokens = va.shape[1]
    row = lax.broadcasted_iota(jnp.int32, (SUBLANES, tokens), 0)
    sums, experts = [], []
    for i in range(PEER_TOPK // 2):
        n = PEER_TOPK // (i + 1)
        width = pl.cdiv(n, SUBLANES) * SUBLANES
        s = va[i:i + 1, :] + vb[:width, :]
        sums.append(s if n == width else jnp.where(row < n, s, -jnp.inf))
        experts.append(ia[i:i + 1, :] * PEER_N_KEYS + ib[:width, :])
    sums.append(va[PEER_TOPK // 2:, :] + vb[0:1, :])
    experts.append(ia[PEER_TOPK // 2:, :] * PEER_N_KEYS + ib[0:1, :])
    return jnp.concatenate(sums, axis=0), jnp.concatenate(experts, axis=0)


def _route_kernel(h_ref, wq_ref, sk_ref, off_ref, gate_ref):
    q = _bdot(h_ref[...], wq_ref[...])
    experts, gates = [], []
    for hd in range(PEER_HEADS):
        vals, ids = [], []
        for p in range(2):
            c0 = (hd * 2 + p) * PEER_HALF
            s_t = lax.dot_general(sk_ref[hd, p], q[:, c0:c0 + PEER_HALF], _NT,
                                  preferred_element_type=jnp.float32)
            v, i = _topk_rows(s_t, PEER_TOPK)
            vals.append(v)
            ids.append(i)
        cand, cand_expert = _pair_candidates(vals[0], vals[1], ids[0], ids[1])
        best, expert = _topk_rows(cand, PEER_TOPK, cand_expert)
        experts.append(expert * PACKED_TILES)
        e = jnp.exp(best - best[0:1, :])
        gates.append(e / jnp.sum(e, axis=0, keepdims=True))
    off_ref[...] = jnp.concatenate(experts, axis=0).T
    gate_ref[...] = jnp.concatenate(gates, axis=0).T


def _route(h, w_q, sub_keys):
    T, D = h.shape
    rows = min(ROUTE_ROWS, T)
    tok = lambda width: pl.BlockSpec((rows, width), lambda i: (i, 0))
    return pl.pallas_call(
        _route_kernel,
        grid=(T // rows,),
        in_specs=[tok(D), pl.BlockSpec(w_q.shape, lambda i: (0, 0)),
                  pl.BlockSpec(sub_keys.shape, lambda i: (0, 0, 0, 0))],
        out_specs=[tok(PEER_SEL), tok(PEER_SEL)],
        out_shape=[jax.ShapeDtypeStruct((T, PEER_SEL), jnp.int32),
                   jax.ShapeDtypeStruct((T, PEER_SEL), jnp.float32)],
        compiler_params=pltpu.CompilerParams(dimension_semantics=("arbitrary",),
                                             vmem_limit_bytes=VMEM_LIMIT),
        name="route",
    )(h, w_q, sub_keys)


def _pack_kernel(w_ref, o_ref):
    def bf16_bits(v):
        return lax.bitcast_convert_type(v.astype(jnp.bfloat16).astype(jnp.float32), jnp.uint32)

    for s in range(PACKED_TILES):
        lo = bf16_bits(w_ref[:, (2 * s) * LANES:(2 * s + 1) * LANES])
        hi = bf16_bits(w_ref[:, (2 * s + 1) * LANES:(2 * s + 2) * LANES])
        o_ref[:, s * LANES:(s + 1) * LANES] = hi | (lo >> 16)


def _pack_table(w):
    experts, D = w.shape
    words = pl.pallas_call(
        _pack_kernel,
        grid=(experts // PACK_ROWS,),
        in_specs=[pl.BlockSpec((PACK_ROWS, D), lambda i: (i, 0))],
        out_specs=pl.BlockSpec((PACK_ROWS, D // 2), lambda i: (i, 0)),
        out_shape=jax.ShapeDtypeStruct((experts, D // 2), jnp.uint32),
        compiler_params=pltpu.CompilerParams(dimension_semantics=("arbitrary",)),
        name="pack",
    )(w)
    return words.reshape(experts * PACKED_TILES, LANES)


def _offset_reader(off_ref, t):
    windows = [off_ref.at[pl.ds(t * PEER_SEL + s, OFFSET_WINDOW)] for s in range(0, PEER_SEL, OFFSET_WINDOW)]
    return lambda j: windows[j // OFFSET_WINDOW][j % OFFSET_WINDOW]


def _table_tile(tab, offset):
    words = tab[pl.ds(pl.multiple_of(offset, PACKED_TILES), PACKED_TILES), :]
    return pltpu.bitcast(words, jnp.bfloat16).astype(jnp.float32)


def _act_kernel(off_ref, h_ref, gate_ref, tab, w_ref, prod, act):
    rows = h_ref.shape[0]

    def products(t0):
        hs = [h_ref[t0 + g] for g in range(ACT_GROUP)]
        offs = [_offset_reader(off_ref, t0 + g) for g in range(ACT_GROUP)]
        for j in range(PEER_SEL):
            for g in range(ACT_GROUP):
                p = _table_tile(tab, offs[g](j)) * hs[g]
                prod[g, j * PACKED_TILES:(j + 1) * PACKED_TILES, :] = p[:PACKED_TILES] + p[PACKED_TILES:]

    def reduce(t0):
        for g in range(ACT_GROUP):
            folded = prod[g, pl.ds(0, PEER_SEL, stride=PACKED_TILES), :]
            for r in range(1, PACKED_TILES):
                folded = folded + prod[g, pl.ds(r, PEER_SEL, stride=PACKED_TILES), :]
            act[pl.ds(t0 + g, 1), :] = jnp.sum(folded.T, axis=0, keepdims=True)

    prod[...] = jnp.zeros(prod.shape, jnp.float32)

    def group(i, carry):
        reduce(jnp.maximum(i - 1, 0) * ACT_GROUP)
        products(i * ACT_GROUP)
        return carry

    lax.fori_loop(0, rows // ACT_GROUP, group, 0)
    reduce(rows - ACT_GROUP)
    w_ref[...] = gate_ref[...] * jax.nn.gelu(act[...])


def _combine_kernel(off_ref, w_ref, x_ref, g_ref, tab, o_ref, splat):
    rows = x_ref.shape[0]

    def spread(t):
        return jnp.broadcast_to(w_ref[pl.ds(t, 1), :], (LANES, PEER_SEL)).T

    for g in range(COMBINE_GROUP):
        splat[g] = spread(g)

    def group(i, carry):
        t0 = i * COMBINE_GROUP
        nxt = [spread(jnp.minimum(t0 + COMBINE_GROUP + g, rows - 1)) for g in range(COMBINE_GROUP)]
        offs = [_offset_reader(off_ref, t0 + g) for g in range(COMBINE_GROUP)]
        acc = [[x_ref[t0 + g]] + [jnp.zeros((ROW_TILES, LANES), jnp.float32)] * (COMBINE_CHAINS - 1)
               for g in range(COMBINE_GROUP)]
        for j in range(PEER_SEL):
            c = j % COMBINE_CHAINS
            for g in range(COMBINE_GROUP):
                w = jnp.broadcast_to(splat[g, j:j + 1, :], (ROW_TILES, LANES))
                acc[g][c] = acc[g][c] + w * _table_tile(tab, offs[g](j))
        for g in range(COMBINE_GROUP):
            o_ref[t0 + g] = sum(acc[g])
            splat[g] = nxt[g]
        return carry

    lax.fori_loop(0, rows // COMBINE_GROUP, group, 0)
    x = o_ref[...]
    ms = jnp.sum(jnp.sum(x * x, axis=2, keepdims=True), axis=1, keepdims=True) / D_MODEL
    o_ref[...] = x * lax.rsqrt(ms + EPS) * g_ref[...]


def _gather_specs(rows, tab):
    flat = pl.BlockSpec((rows * PEER_SEL,), lambda i: (i,), memory_space=pltpu.SMEM)
    tiles = pl.BlockSpec((rows, ROW_TILES, LANES), lambda i: (i, 0, 0))
    sel = pl.BlockSpec((rows, PEER_SEL), lambda i: (i, 0))
    table = pl.BlockSpec(tab.shape, lambda i: (0, 0), pipeline_mode=pl.Buffered(1))
    return flat, tiles, sel, table


def _act(off_flat, h3, gate, tab):
    T = h3.shape[0]
    rows = min(GATHER_ROWS, T)
    flat, tiles, sel, table = _gather_specs(rows, tab)
    prod = pltpu.VMEM((ACT_GROUP, PEER_SEL * PACKED_TILES, LANES), jnp.float32)
    return pl.pallas_call(
        _act_kernel,
        grid=(T // rows,),
        in_specs=[flat, tiles, sel, table],
        out_specs=sel,
        out_shape=jax.ShapeDtypeStruct((T, PEER_SEL), jnp.float32),
        scratch_shapes=[prod, pltpu.VMEM((rows, PEER_SEL), jnp.float32)],
        compiler_params=pltpu.CompilerParams(dimension_semantics=("arbitrary",),
                                             vmem_limit_bytes=VMEM_LIMIT),
        name="act",
    )(off_flat, h3, gate, tab)


def _combine(off_flat, w, x3, g3, tab):
    T = x3.shape[0]
    rows = min(GATHER_ROWS, T)
    flat, tiles, sel, table = _gather_specs(rows, tab)
    return pl.pallas_call(
        _combine_kernel,
        grid=(T // rows,),
        in_specs=[flat, sel, tiles, pl.BlockSpec((1, ROW_TILES, LANES), lambda i: (0, 0, 0)), table],
        out_specs=tiles,
        out_shape=jax.ShapeDtypeStruct(x3.shape, jnp.float32),
        scratch_shapes=[pltpu.VMEM((COMBINE_GROUP, PEER_SEL, LANES), jnp.float32)],
        compiler_params=pltpu.CompilerParams(dimension_semantics=("arbitrary",),
                                             vmem_limit_bytes=VMEM_LIMIT),
        name="combine",
    )(off_flat, w, x3, g3, tab)


def kernel(x, mix_norm_g, w_in, pool_w, pool_scale, w_branch_a, conv_w, conv_b, conv_ln_g, conv_ln_b, w_branch_b, gate_b, w_out, ffn_norm_g, peer_w_q, peer_sub_keys, peer_u, peer_v, final_norm_g):
    B, S, D = x.shape
    T = B * S
    assert mix_norm_g.shape[0] == 1, "the combine kernel fuses the final RMSNorm, so exactly one layer"
    bf = lambda w: w[0].astype(jnp.bfloat16)
    x1, h = _mixer(x, mix_norm_g[0], bf(w_in), bf(pool_w), pool_scale[0], bf(w_branch_a), conv_w[0], conv_b[0],
                   conv_ln_g[0], conv_ln_b[0], bf(w_branch_b), gate_b[0], bf(w_out), ffn_norm_g[0])
    h = h.reshape(T, D)
    off, gate = _route(h, bf(peer_w_q), peer_sub_keys[0])
    off_flat = off.reshape(T * PEER_SEL)
    w = _act(off_flat, h.reshape(T, ROW_TILES, LANES), gate, _pack_table(peer_u[0]))
    out = _combine(off_flat, w, x1.reshape(T, ROW_TILES, LANES),
                   final_norm_g.reshape(1, ROW_TILES, LANES), _pack_table(peer_v[0]))
    return out.reshape(B, S, D)
```

```python
import jax
import jax.numpy as jnp
from jax import lax
from jax.experimental import pallas as pl
from jax.experimental.pallas import tpu as pltpu

D_MODEL = 1024
POOL_WIDTH = 512
POOL_GROUPS = 4
POOL_GROUP_DIM = 128
POOL_WINDOWS = (2, 4, 8, 16)
CONV_WIDTH = 512
CONV_KERNEL = 31
PEER_HEADS = 8
PEER_N_KEYS = 128
PEER_TOPK = 16
PEER_HALF = 128
PEER_SEL = PEER_HEADS * PEER_TOPK
EPS = 1e-6

LANES = 128
SUBLANES = 8
ROW_TILES = D_MODEL // LANES
PACKED_TILES = ROW_TILES // 2
HALO = 32
VMEM_LIMIT = 56 * 1024 * 1024

MIX_ROWS = 256
CONV_CHUNK = 32
ROUTE_ROWS = 128
GATHER_ROWS = 256
ACT_GROUP = 4
OFFSET_WINDOW = 8
PACK_ROWS = 512
COMBINE_GROUP = 2
COMBINE_CHAINS = 2

_NT = (((1,), (1,)), ((), ()))


def _rms(x, g):
    return x * lax.rsqrt(jnp.mean(x * x, axis=-1, keepdims=True) + EPS) * g


def _bdot(a, b):
    return jnp.dot(a.astype(jnp.bfloat16), b, preferred_element_type=jnp.float32)


def _mixer_kernel(x_ref, g1_ref, win_ref, poolw_ref, pscale_ref, wa_ref, convw_ref, convb_ref,
                  lng_ref, lnb_ref, wb_ref, gateb_ref, wout_ref, g2_ref,
                  x1_ref, h_ref, zbuf, ubuf, ybuf, ushift):
    s = pl.program_id(1)
    rows = x_ref.shape[0]

    @pl.when(s == 0)
    def _():
        zbuf[0:HALO, :] = jnp.zeros((HALO, POOL_WIDTH), jnp.float32)
        ubuf[0:HALO, :] = jnp.zeros((HALO, CONV_WIDTH), jnp.float32)

    x = x_ref[...]
    proj = _bdot(_rms(x, g1_ref[...]), win_ref[...])
    za = proj[:, :POOL_WIDTH]
    zb = proj[:, POOL_WIDTH:POOL_WIDTH + 2 * CONV_WIDTH]
    zg = proj[:, POOL_WIDTH + 2 * CONV_WIDTH:]

    zbuf[HALO:HALO + rows, :] = za
    pos = (s * rows + lax.broadcasted_iota(jnp.int32, (rows, 1), 0)).astype(jnp.float32)
    pa = []
    for gi, w in enumerate(POOL_WINDOWS):
        cols = slice(gi * POOL_GROUP_DIM, (gi + 1) * POOL_GROUP_DIM)
        zcur = za[:, cols]
        acc = zcur
        for i in range(1, w):
            acc = acc + zbuf[HALO - i:HALO - i + rows, cols]
        pooled = acc / jnp.minimum(pos + 1.0, float(w))
        pa.append(_bdot(pooled - zcur, poolw_ref[gi]))
    pa = jnp.concatenate(pa, axis=1) * pscale_ref[...]
    ya = _bdot(pa, wa_ref[...])

    ubuf[HALO:HALO + rows, :] = zb[:, :CONV_WIDTH] * jax.nn.sigmoid(zb[:, CONV_WIDTH:])
    span = rows + HALO - SUBLANES
    for r in range(1, SUBLANES):
        ushift[r - 1, 0:span, :] = ubuf[r:r + span, :]
    for c in range(rows // CONV_CHUNK):
        acc = jnp.broadcast_to(convb_ref[...], (CONV_CHUNK, CONV_WIDTH))
        for k in range(CONV_KERNEL):
            start = c * CONV_CHUNK + HALO - (CONV_KERNEL - 1) + k
            r, base = start % SUBLANES, start - start % SUBLANES
            taps = ubuf if r == 0 else ushift.at[r - 1]
            acc = acc + convw_ref[k:k + 1, :] * taps[base:base + CONV_CHUNK, :]
        ybuf[c * CONV_CHUNK:(c + 1) * CONV_CHUNK, :] = acc
    y = ybuf[...]
    mu = jnp.mean(y, axis=-1, keepdims=True)
    yc = y - mu
    var = jnp.mean(yc * yc, axis=-1, keepdims=True)
    yn = yc * lax.rsqrt(var + EPS) * lng_ref[...] + lnb_ref[...]
    yb = _bdot(yn * jax.nn.sigmoid(yn), wb_ref[...])

    gates = jax.nn.sigmoid(zg + gateb_ref[...])
    merged = gates[:, :D_MODEL] * ya + gates[:, D_MODEL:] * yb
    x1 = x + _bdot(merged, wout_ref[...])
    x1_ref[...] = x1
    h_ref[...] = _rms(x1, g2_ref[...])

    zbuf[0:HALO, :] = zbuf[rows:rows + HALO, :]
    ubuf[0:HALO, :] = ubuf[rows:rows + HALO, :]


def _mixer(x, g1, w_in, pool_w, pool_scale, w_a, conv_w, conv_b, ln_g, ln_b, w_b, gate_b, w_out, g2):
    B, S, D = x.shape
    rows = min(MIX_ROWS, S)
    const = lambda shape: pl.BlockSpec(shape, lambda b, s: (0,) * len(shape))
    tok = pl.BlockSpec((None, rows, D), lambda b, s: (b, s, 0))
    return pl.pallas_call(
        _mixer_kernel,
        grid=(B, S // rows),
        in_specs=[tok, const((1, D)), const(w_in.shape), const(pool_w.shape), const((1, POOL_WIDTH)),
                  const(w_a.shape), const(conv_w.shape), const((1, CONV_WIDTH)), const((1, CONV_WIDTH)),
                  const((1, CONV_WIDTH)), const(w_b.shape), const((1, 2 * D)), const(w_out.shape),
                  const((1, D))],
        out_specs=[tok, tok],
        out_shape=[jax.ShapeDtypeStruct((B, S, D), jnp.float32)] * 2,
        scratch_shapes=[pltpu.VMEM((HALO + rows, POOL_WIDTH), jnp.float32),
                        pltpu.VMEM((HALO + rows, CONV_WIDTH), jnp.float32),
                        pltpu.VMEM((rows, CONV_WIDTH), jnp.float32),
                        pltpu.VMEM((SUBLANES - 1, HALO + rows, CONV_WIDTH), jnp.float32)],
        compiler_params=pltpu.CompilerParams(dimension_semantics=("arbitrary", "arbitrary"),
                                             vmem_limit_bytes=VMEM_LIMIT),
        name="mixer",
    )(x, g1.reshape(1, D), w_in, pool_w, pool_scale.reshape(1, -1), w_a, conv_w, conv_b.reshape(1, -1),
      ln_g.reshape(1, -1), ln_b.reshape(1, -1), w_b, gate_b.reshape(1, -1), w_out, g2.reshape(1, D))


def _topk_rows(s, k, payload=None):
    n = s.shape[0]
    iota = lax.broadcasted_iota(jnp.int32, s.shape, 0).astype(jnp.float32)
    vals, picks = [], []
    for _ in range(k):
        m = jnp.max(s, axis=0, keepdims=True)
        i = jnp.min(jnp.where(s == m, iota, float(n)), axis=0, keepdims=True)
        hit = iota == i
        vals.append(m)
        picks.append(i if payload is None else jnp.max(jnp.where(hit, payload, -1.0), axis=0, keepdims=True))
        s = jnp.where(hit, -jnp.inf, s)
    return jnp.concatenate(vals, axis=0), jnp.concatenate(picks, axis=0)


def _pair_candidates(va, vb, ia, ib):
    tokens = va.shape[1]
    row = lax.broadcasted_iota(jnp.int32, (SUBLANES, tokens), 0)
    sums, experts = [], []
    for i in range(PEER_TOPK // 2):
        n = PEER_TOPK // (i + 1)
        width = pl.cdiv(n, SUBLANES) * SUBLANES
        s = va[i:i + 1, :] + vb[:width, :]
        sums.append(s if n == width else jnp.where(row < n, s, -jnp.inf))
        experts.append(ia[i:i + 1, :] * PEER_N_KEYS + ib[:width, :])
    sums.append(va[PEER_TOPK // 2:, :] + vb[0:1, :])
    experts.append(ia[PEER_TOPK // 2:, :] * PEER_N_KEYS + ib[0:1, :])
    return jnp.concatenate(sums, axis=0), jnp.concatenate(experts, axis=0)


def _route_kernel(h_ref, wq_ref, sk_ref, off_ref, gate_ref):
    q = _bdot(h_ref[...], wq_ref[...])
    experts, gates = [], []
    for hd in range(PEER_HEADS):
        vals, ids = [], []
        for p in range(2):
            c0 = (hd * 2 + p) * PEER_HALF
            s_t = lax.dot_general(sk_ref[hd, p], q[:, c0:c0 + PEER_HALF], _NT,
                                  preferred_element_type=jnp.float32)
            v, i = _topk_rows(s_t, PEER_TOPK)
            vals.append(v)
            ids.append(i)
        cand, cand_expert = _pair_candidates(vals[0], vals[1], ids[0], ids[1])
        best, expert = _topk_rows(cand, PEER_TOPK, cand_expert)
        experts.append(expert * PACKED_TILES)
        e = jnp.exp(best - best[0:1, :])
        gates.append(e / jnp.sum(e, axis=0, keepdims=True))
    off_ref[...] = jnp.concatenate(experts, axis=0).T.astype(jnp.int32)
    gate_ref[...] = jnp.concatenate(gates, axis=0).T


def _route(h, w_q, sub_keys):
    T, D = h.shape
    rows = min(ROUTE_ROWS, T)
    tok = lambda width: pl.BlockSpec((rows, width), lambda i: (i, 0))
    return pl.pallas_call(
        _route_kernel,
        grid=(T // rows,),
        in_specs=[tok(D), pl.BlockSpec(w_q.shape, lambda i: (0, 0)),
                  pl.BlockSpec(sub_keys.shape, lambda i: (0, 0, 0, 0))],
        out_specs=[tok(PEER_SEL), tok(PEER_SEL)],
        out_shape=[jax.ShapeDtypeStruct((T, PEER_SEL), jnp.int32),
                   jax.ShapeDtypeStruct((T, PEER_SEL), jnp.float32)],
        compiler_params=pltpu.CompilerParams(dimension_semantics=("arbitrary",),
                                             vmem_limit_bytes=VMEM_LIMIT),
        name="route",
    )(h, w_q, sub_keys)


def _pack_kernel(w_ref, o_ref):
    def bf16_bits(v):
        return lax.bitcast_convert_type(v.astype(jnp.bfloat16).astype(jnp.float32), jnp.uint32)

    for s in range(PACKED_TILES):
        lo = bf16_bits(w_ref[:, (2 * s) * LANES:(2 * s + 1) * LANES])
        hi = bf16_bits(w_ref[:, (2 * s + 1) * LANES:(2 * s + 2) * LANES])
        o_ref[:, s * LANES:(s + 1) * LANES] = hi | (lo >> 16)


def _pack_table(w):
    experts, D = w.shape
    words = pl.pallas_call(
        _pack_kernel,
        grid=(experts // PACK_ROWS,),
        in_specs=[pl.BlockSpec((PACK_ROWS, D), lambda i: (i, 0))],
        out_specs=pl.BlockSpec((PACK_ROWS, D // 2), lambda i: (i, 0)),
        out_shape=jax.ShapeDtypeStruct((experts, D // 2), jnp.uint32),
        compiler_params=pltpu.CompilerParams(dimension_semantics=("arbitrary",)),
        name="pack",
    )(w)
    return words.reshape(experts * PACKED_TILES, LANES)


def _offset_reader(off_ref, t):
    windows = [off_ref.at[pl.ds(t * PEER_SEL + s, OFFSET_WINDOW)] for s in range(0, PEER_SEL, OFFSET_WINDOW)]
    return lambda j: windows[j // OFFSET_WINDOW][j % OFFSET_WINDOW]


def _table_tile(tab, offset):
    words = tab[pl.ds(pl.multiple_of(offset, PACKED_TILES), PACKED_TILES), :]
    return pltpu.bitcast(words, jnp.bfloat16).astype(jnp.float32)


def _act_kernel(off_ref, h_ref, gate_ref, tab, w_ref, prod, act):
    rows = h_ref.shape[0]

    def products(t0):
        hs = [h_ref[t0 + g] for g in range(ACT_GROUP)]
        offs = [_offset_reader(off_ref, t0 + g) for g in range(ACT_GROUP)]
        for j in range(PEER_SEL):
            for g in range(ACT_GROUP):
                p = _table_tile(tab, offs[g](j)) * hs[g]
                prod[g, j * PACKED_TILES:(j + 1) * PACKED_TILES, :] = p[:PACKED_TILES] + p[PACKED_TILES:]

    def reduce(t0):
        for g in range(ACT_GROUP):
            folded = prod[g, pl.ds(0, PEER_SEL, stride=PACKED_TILES), :]
            for r in range(1, PACKED_TILES):
                folded = folded + prod[g, pl.ds(r, PEER_SEL, stride=PACKED_TILES), :]
            act[pl.ds(t0 + g, 1), :] = jnp.sum(folded.T, axis=0, keepdims=True)

    prod[...] = jnp.zeros(prod.shape, jnp.float32)

    def group(i, carry):
        reduce(jnp.maximum(i - 1, 0) * ACT_GROUP)
        products(i * ACT_GROUP)
        return carry

    lax.fori_loop(0, rows // ACT_GROUP, group, 0)
    reduce(rows - ACT_GROUP)
    w_ref[...] = gate_ref[...] * jax.nn.gelu(act[...])


def _combine_kernel(off_ref, w_ref, x_ref, g_ref, tab, o_ref, splat):
    rows = x_ref.shape[0]

    def spread(t):
        return jnp.broadcast_to(w_ref[pl.ds(t, 1), :], (LANES, PEER_SEL)).T

    for g in range(COMBINE_GROUP):
        splat[g] = spread(g)

    def group(i, carry):
        t0 = i * COMBINE_GROUP
        nxt = [spread(jnp.minimum(t0 + COMBINE_GROUP + g, rows - 1)) for g in range(COMBINE_GROUP)]
        offs = [_offset_reader(off_ref, t0 + g) for g in range(COMBINE_GROUP)]
        acc = [[x_ref[t0 + g]] + [jnp.zeros((ROW_TILES, LANES), jnp.float32)] * (COMBINE_CHAINS - 1)
               for g in range(COMBINE_GROUP)]
        for j in range(PEER_SEL):
            c = j % COMBINE_CHAINS
            for g in range(COMBINE_GROUP):
                w = jnp.broadcast_to(splat[g, j:j + 1, :], (ROW_TILES, LANES))
                acc[g][c] = acc[g][c] + w * _table_tile(tab, offs[g](j))
        for g in range(COMBINE_GROUP):
            o_ref[t0 + g] = sum(acc[g])
            splat[g] = nxt[g]
        return carry

    lax.fori_loop(0, rows // COMBINE_GROUP, group, 0)
    x = o_ref[...]
    ms = jnp.sum(jnp.sum(x * x, axis=2, keepdims=True), axis=1, keepdims=True) / D_MODEL
    o_ref[...] = x * lax.rsqrt(ms + EPS) * g_ref[...]


def _gather_specs(rows, tab):
    flat = pl.BlockSpec((rows * PEER_SEL,), lambda i: (i,), memory_space=pltpu.SMEM)
    tiles = pl.BlockSpec((rows, ROW_TILES, LANES), lambda i: (i, 0, 0))
    sel = pl.BlockSpec((rows, PEER_SEL), lambda i: (i, 0))
    table = pl.BlockSpec(tab.shape, lambda i: (0, 0), pipeline_mode=pl.Buffered(1))
    return flat, tiles, sel, table


def _act(off_flat, h3, gate, tab):
    T = h3.shape[0]
    rows = min(GATHER_ROWS, T)
    flat, tiles, sel, table = _gather_specs(rows, tab)
    prod = pltpu.VMEM((ACT_GROUP, PEER_SEL * PACKED_TILES, LANES), jnp.float32)
    return pl.pallas_call(
        _act_kernel,
        grid=(T // rows,),
        in_specs=[flat, tiles, sel, table],
        out_specs=sel,
        out_shape=jax.ShapeDtypeStruct((T, PEER_SEL), jnp.float32),
        scratch_shapes=[prod, pltpu.VMEM((rows, PEER_SEL), jnp.float32)],
        compiler_params=pltpu.CompilerParams(dimension_semantics=("arbitrary",),
                                             vmem_limit_bytes=VMEM_LIMIT),
        name="act",
    )(off_flat, h3, gate, tab)


def _combine(off_flat, w, x3, g3, tab):
    T = x3.shape[0]
    rows = min(GATHER_ROWS, T)
    flat, tiles, sel, table = _gather_specs(rows, tab)
    return pl.pallas_call(
        _combine_kernel,
        grid=(T // rows,),
        in_specs=[flat, sel, tiles, pl.BlockSpec((1, ROW_TILES, LANES), lambda i: (0, 0, 0)), table],
        out_specs=tiles,
        out_shape=jax.ShapeDtypeStruct(x3.shape, jnp.float32),
        scratch_shapes=[pltpu.VMEM((COMBINE_GROUP, PEER_SEL, LANES), jnp.float32)],
        compiler_params=pltpu.CompilerParams(dimension_semantics=("arbitrary",),
                                             vmem_limit_bytes=VMEM_LIMIT),
        name="combine",
    )(off_flat, w, x3, g3, tab)


def kernel(x, mix_norm_g, w_in, pool_w, pool_scale, w_branch_a, conv_w, conv_b, conv_ln_g, conv_ln_b, w_branch_b, gate_b, w_out, ffn_norm_g, peer_w_q, peer_sub_keys, peer_u, peer_v, final_norm_g):
    B, S, D = x.shape
    T = B * S
    assert mix_norm_g.shape[0] == 1, "the combine kernel fuses the final RMSNorm, so exactly one layer"
    bf = lambda w: w[0].astype(jnp.bfloat16)
    x1, h = _mixer(x, mix_norm_g[0], bf(w_in), bf(pool_w), pool_scale[0], bf(w_branch_a), conv_w[0], conv_b[0],
                   conv_ln_g[0], conv_ln_b[0], bf(w_branch_b), gate_b[0], bf(w_out), ffn_norm_g[0])
    h = h.reshape(T, D)
    off, gate = _route(h, bf(peer_w_q), peer_sub_keys[0])
    off_flat = off.reshape(T * PEER_SEL)
    w = _act(off_flat, h.reshape(T, ROW_TILES, LANES), gate, _pack_table(peer_u[0]))
    out = _combine(off_flat, w, x1.reshape(T, ROW_TILES, LANES),
                   final_norm_g.reshape(1, ROW_TILES, LANES), _pack_table(peer_v[0]))
    return out.reshape(B, S, D)
```

```python
import jax
import jax.numpy as jnp
from jax import lax
from jax.experimental import pallas as pl
from jax.experimental.pallas import tpu as pltpu

D_MODEL = 1024
POOL_WIDTH = 512
POOL_GROUPS = 4
POOL_GROUP_DIM = 128
POOL_WINDOWS = (2, 4, 8, 16)
CONV_WIDTH = 512
CONV_KERNEL = 31
PEER_HEADS = 8
PEER_N_KEYS = 128
PEER_TOPK = 16
PEER_HALF = 128
PEER_SEL = PEER_HEADS * PEER_TOPK
EPS = 1e-6

LANES = 128
SUBLANES = 8
ROW_TILES = D_MODEL // LANES
PACKED_TILES = ROW_TILES // 2
HALO = 32
VMEM_LIMIT = 56 * 1024 * 1024

MIX_ROWS = 256
CONV_CHUNK = 32
ROUTE_ROWS = 128
GATHER_ROWS = 256
ACT_GROUP = 4
OFFSET_WINDOW = 8
PACK_ROWS = 512
COMBINE_GROUP = 2
COMBINE_CHAINS = 2

_NT = (((1,), (1,)), ((), ()))


def _rms(x, g):
    return x * lax.rsqrt(jnp.mean(x * x, axis=-1, keepdims=True) + EPS) * g


def _bdot(a, b):
    return jnp.dot(a.astype(jnp.bfloat16), b, preferred_element_type=jnp.float32)


def _mixer_kernel(x_ref, g1_ref, win_ref, poolw_ref, pscale_ref, wa_ref, convw_ref, convb_ref,
                  lng_ref, lnb_ref, wb_ref, gateb_ref, wout_ref, g2_ref,
                  x1_ref, h_ref, zbuf, ubuf, ybuf, ushift):
    s = pl.program_id(1)
    rows = x_ref.shape[0]

    @pl.when(s == 0)
    def _():
        zbuf[0:HALO, :] = jnp.zeros((HALO, POOL_WIDTH), jnp.float32)
        ubuf[0:HALO, :] = jnp.zeros((HALO, CONV_WIDTH), jnp.float32)

    x = x_ref[...]
    proj = _bdot(_rms(x, g1_ref[...]), win_ref[...])
    za = proj[:, :POOL_WIDTH]
    zb = proj[:, POOL_WIDTH:POOL_WIDTH + 2 * CONV_WIDTH]
    zg = proj[:, POOL_WIDTH + 2 * CONV_WIDTH:]

    zbuf[HALO:HALO + rows, :] = za
    pos = (s * rows + lax.broadcasted_iota(jnp.int32, (rows, 1), 0)).astype(jnp.float32)
    pa = []
    for gi, w in enumerate(POOL_WINDOWS):
        cols = slice(gi * POOL_GROUP_DIM, (gi + 1) * POOL_GROUP_DIM)
        zcur = za[:, cols]
        acc = zcur
        for i in range(1, w):
            acc = acc + zbuf[HALO - i:HALO - i + rows, cols]
        pooled = acc / jnp.minimum(pos + 1.0, float(w))
        pa.append(_bdot(pooled - zcur, poolw_ref[gi]))
    pa = jnp.concatenate(pa, axis=1) * pscale_ref[...]
    ya = _bdot(pa, wa_ref[...])

    ubuf[HALO:HALO + rows, :] = zb[:, :CONV_WIDTH] * jax.nn.sigmoid(zb[:, CONV_WIDTH:])
    span = rows + HALO - SUBLANES
    for r in range(1, SUBLANES):
        ushift[r - 1, 0:span, :] = ubuf[r:r + span, :]
    for c in range(rows // CONV_CHUNK):
        acc = jnp.broadcast_to(convb_ref[...], (CONV_CHUNK, CONV_WIDTH))
        for k in range(CONV_KERNEL):
            start = c * CONV_CHUNK + HALO - (CONV_KERNEL - 1) + k
            r, base = start % SUBLANES, start - start % SUBLANES
            taps = ubuf if r == 0 else ushift.at[r - 1]
            acc = acc + convw_ref[k:k + 1, :] * taps[base:base + CONV_CHUNK, :]
        ybuf[c * CONV_CHUNK:(c + 1) * CONV_CHUNK, :] = acc
    y = ybuf[...]
    mu = jnp.mean(y, axis=-1, keepdims=True)
    yc = y - mu
    var = jnp.mean(yc * yc, axis=-1, keepdims=True)
    yn = yc * lax.rsqrt(var + EPS) * lng_ref[...] + lnb_ref[...]
    yb = _bdot(yn * jax.nn.sigmoid(yn), wb_ref[...])

    gates = jax.nn.sigmoid(zg + gateb_ref[...])
    merged = gates[:, :D_MODEL] * ya + gates[:, D_MODEL:] * yb
    x1 = x + _bdot(merged, wout_ref[...])
    x1_ref[...] = x1
    h_ref[...] = _rms(x1, g2_ref[...])

    zbuf[0:HALO, :] = zbuf[rows:rows + HALO, :]
    ubuf[0:HALO, :] = ubuf[rows:rows + HALO, :]


def _mixer(x, g1, w_in, pool_w, pool_scale, w_a, conv_w, conv_b, ln_g, ln_b, w_b, gate_b, w_out, g2):
    B, S, D = x.shape
    rows = min(MIX_ROWS, S)
    const = lambda shape: pl.BlockSpec(shape, lambda b, s: (0,) * len(shape))
    tok = pl.BlockSpec((None, rows, D), lambda b, s: (b, s, 0))
    return pl.pallas_call(
        _mixer_kernel,
        grid=(B, S // rows),
        in_specs=[tok, const((1, D)), const(w_in.shape), const(pool_w.shape), const((1, POOL_WIDTH)),
                  const(w_a.shape), const(conv_w.shape), const((1, CONV_WIDTH)), const((1, CONV_WIDTH)),
                  const((1, CONV_WIDTH)), const(w_b.shape), const((1, 2 * D)), const(w_out.shape),
                  const((1, D))],
        out_specs=[tok, tok],
        out_shape=[jax.ShapeDtypeStruct((B, S, D), jnp.float32)] * 2,
        scratch_shapes=[pltpu.VMEM((HALO + rows, POOL_WIDTH), jnp.float32),
                        pltpu.VMEM((HALO + rows, CONV_WIDTH), jnp.float32),
                        pltpu.VMEM((rows, CONV_WIDTH), jnp.float32),
                        pltpu.VMEM((SUBLANES - 1, HALO + rows, CONV_WIDTH), jnp.float32)],
        compiler_params=pltpu.CompilerParams(dimension_semantics=("arbitrary", "arbitrary"),
                                             vmem_limit_bytes=VMEM_LIMIT),
        name="mixer",
    )(x, g1.reshape(1, D), w_in, pool_w, pool_scale.reshape(1, -1), w_a, conv_w, conv_b.reshape(1, -1),
      ln_g.reshape(1, -1), ln_b.reshape(1, -1), w_b, gate_b.reshape(1, -1), w_out, g2.reshape(1, D))


def _topk_rows(s, k, payload=None):
    n = s.shape[0]
    iota = lax.broadcasted_iota(jnp.int32, s.shape, 0).astype(jnp.float32)
    vals, picks = [], []
    for _ in range(k):
        m = jnp.max(s, axis=0, keepdims=True)
        i = jnp.min(jnp.where(s == m, iota, float(n)), axis=0, keepdims=True)
        hit = iota == i
        vals.append(m)
        picks.append(i if payload is None else jnp.max(jnp.where(hit, payload, -1.0), axis=0, keepdims=True))
        s = jnp.where(hit, -jnp.inf, s)
    return jnp.concatenate(vals, axis=0), jnp.concatenate(picks, axis=0)


def _topk_keys(s, k):
    n, tokens = s.shape
    tiles = n // SUBLANES
    assert k == tiles
    row = lax.broadcasted_iota(jnp.int32, (SUBLANES, tokens), 0).astype(jnp.float32)
    vals = [s[v * SUBLANES:(v + 1) * SUBLANES, :] for v in range(tiles)]
    idx = [row + float(v * SUBLANES) for v in range(tiles)]
    for rnd in range(tiles):
        for p in range(rnd % 2, tiles - 1, 2):
            swap = vals[p] < vals[p + 1]
            vals[p], vals[p + 1] = jnp.where(swap, vals[p + 1], vals[p]), jnp.where(swap, vals[p], vals[p + 1])
            idx[p], idx[p + 1] = jnp.where(swap, idx[p + 1], idx[p]), jnp.where(swap, idx[p], idx[p + 1])
    out_v, out_i = [], []
    for r in range(k):
        m = jnp.max(vals[0], axis=0, keepdims=True)
        i = jnp.min(jnp.where(vals[0] == m, idx[0], float(n)), axis=0, keepdims=True)
        hit = idx[0] == i
        out_v.append(m)
        out_i.append(i)
        for lvl in range(k - r - 1):
            vals[lvl] = jnp.where(hit, vals[lvl + 1], vals[lvl])
            idx[lvl] = jnp.where(hit, idx[lvl + 1], idx[lvl])
    return jnp.concatenate(out_v, axis=0), jnp.concatenate(out_i, axis=0)


def _pair_candidates(va, vb, ia, ib):
    tokens = va.shape[1]
    row = lax.broadcasted_iota(jnp.int32, (SUBLANES, tokens), 0)
    sums, experts = [], []
    for i in range(PEER_TOPK // 2):
        n = PEER_TOPK // (i + 1)
        width = pl.cdiv(n, SUBLANES) * SUBLANES
        s = va[i:i + 1, :] + vb[:width, :]
        sums.append(s if n == width else jnp.where(row < n, s, -jnp.inf))
        experts.append(ia[i:i + 1, :] * PEER_N_KEYS + ib[:width, :])
    sums.append(va[PEER_TOPK // 2:, :] + vb[0:1, :])
    experts.append(ia[PEER_TOPK // 2:, :] * PEER_N_KEYS + ib[0:1, :])
    return jnp.concatenate(sums, axis=0), jnp.concatenate(experts, axis=0)


def _route_kernel(h_ref, wq_ref, sk_ref, off_ref, gate_ref):
    q = _bdot(h_ref[...], wq_ref[...])
    experts, gates = [], []
    for hd in range(PEER_HEADS):
        vals, ids = [], []
        for p in range(2):
            c0 = (hd * 2 + p) * PEER_HALF
            s_t = lax.dot_general(sk_ref[hd, p], q[:, c0:c0 + PEER_HALF], _NT,
                                  preferred_element_type=jnp.float32)
            v, i = _topk_keys(s_t, PEER_TOPK)
            vals.append(v)
            ids.append(i)
        cand, cand_expert = _pair_candidates(vals[0], vals[1], ids[0], ids[1])
        best, expert = _topk_rows(cand, PEER_TOPK, cand_expert)
        experts.append(expert * PACKED_TILES)
        e = jnp.exp(best - best[0:1, :])
        gates.append(e / jnp.sum(e, axis=0, keepdims=True))
    off_ref[...] = jnp.concatenate(experts, axis=0).T.astype(jnp.int32)
    gate_ref[...] = jnp.concatenate(gates, axis=0).T


def _route(h, w_q, sub_keys):
    T, D = h.shape
    rows = min(ROUTE_ROWS, T)
    tok = lambda width: pl.BlockSpec((rows, width), lambda i: (i, 0))
    return pl.pallas_call(
        _route_kernel,
        grid=(T // rows,),
        in_specs=[tok(D), pl.BlockSpec(w_q.shape, lambda i: (0, 0)),
                  pl.BlockSpec(sub_keys.shape, lambda i: (0, 0, 0, 0))],
        out_specs=[tok(PEER_SEL), tok(PEER_SEL)],
        out_shape=[jax.ShapeDtypeStruct((T, PEER_SEL), jnp.int32),
                   jax.ShapeDtypeStruct((T, PEER_SEL), jnp.float32)],
        compiler_params=pltpu.CompilerParams(dimension_semantics=("arbitrary",),
                                             vmem_limit_bytes=VMEM_LIMIT),
        name="route",
    )(h, w_q, sub_keys)


def _pack_kernel(w_ref, o_ref):
    def bf16_bits(v):
        return lax.bitcast_convert_type(v.astype(jnp.bfloat16).astype(jnp.float32), jnp.uint32)

    for s in range(PACKED_TILES):
        lo = bf16_bits(w_ref[:, (2 * s) * LANES:(2 * s + 1) * LANES])
        hi = bf16_bits(w_ref[:, (2 * s + 1) * LANES:(2 * s + 2) * LANES])
        o_ref[:, s * LANES:(s + 1) * LANES] = hi | (lo >> 16)


def _pack_table(w):
    experts, D = w.shape
    words = pl.pallas_call(
        _pack_kernel,
        grid=(experts // PACK_ROWS,),
        in_specs=[pl.BlockSpec((PACK_ROWS, D), lambda i: (i, 0))],
        out_specs=pl.BlockSpec((PACK_ROWS, D // 2), lambda i: (i, 0)),
        out_shape=jax.ShapeDtypeStruct((experts, D // 2), jnp.uint32),
        compiler_params=pltpu.CompilerParams(dimension_semantics=("arbitrary",)),
        name="pack",
    )(w)
    return words.reshape(experts * PACKED_TILES, LANES)


def _offset_reader(off_ref, t):
    windows = [off_ref.at[pl.ds(t * PEER_SEL + s, OFFSET_WINDOW)] for s in range(0, PEER_SEL, OFFSET_WINDOW)]
    return lambda j: windows[j // OFFSET_WINDOW][j % OFFSET_WINDOW]


def _table_tile(tab, offset):
    words = tab[pl.ds(pl.multiple_of(offset, PACKED_TILES), PACKED_TILES), :]
    return pltpu.bitcast(words, jnp.bfloat16).astype(jnp.float32)


def _act_kernel(off_ref, h_ref, gate_ref, tab, w_ref, prod, act):
    rows = h_ref.shape[0]

    def products(t0):
        hs = [h_ref[t0 + g] for g in range(ACT_GROUP)]
        offs = [_offset_reader(off_ref, t0 + g) for g in range(ACT_GROUP)]
        for j in range(PEER_SEL):
            for g in range(ACT_GROUP):
                p = _table_tile(tab, offs[g](j)) * hs[g]
                prod[g, j * PACKED_TILES:(j + 1) * PACKED_TILES, :] = p[:PACKED_TILES] + p[PACKED_TILES:]

    def reduce(t0):
        for g in range(ACT_GROUP):
            folded = prod[g, pl.ds(0, PEER_SEL, stride=PACKED_TILES), :]
            for r in range(1, PACKED_TILES):
                folded = folded + prod[g, pl.ds(r, PEER_SEL, stride=PACKED_TILES), :]
            act[pl.ds(t0 + g, 1), :] = jnp.sum(folded.T, axis=0, keepdims=True)

    prod[...] = jnp.zeros(prod.shape, jnp.float32)

    def group(i, carry):
        reduce(jnp.maximum(i - 1, 0) * ACT_GROUP)
        products(i * ACT_GROUP)
        return carry

    lax.fori_loop(0, rows // ACT_GROUP, group, 0)
    reduce(rows - ACT_GROUP)
    w_ref[...] = gate_ref[...] * jax.nn.gelu(act[...])


def _combine_kernel(off_ref, w_ref, x_ref, g_ref, tab, o_ref, splat):
    rows = x_ref.shape[0]

    def spread(t):
        return jnp.broadcast_to(w_ref[pl.ds(t, 1), :], (LANES, PEER_SEL)).T

    for g in range(COMBINE_GROUP):
        splat[g] = spread(g)

    def group(i, carry):
        t0 = i * COMBINE_GROUP
        nxt = [spread(jnp.minimum(t0 + COMBINE_GROUP + g, rows - 1)) for g in range(COMBINE_GROUP)]
        offs = [_offset_reader(off_ref, t0 + g) for g in range(COMBINE_GROUP)]
        acc = [[x_ref[t0 + g]] + [jnp.zeros((ROW_TILES, LANES), jnp.float32)] * (COMBINE_CHAINS - 1)
               for g in range(COMBINE_GROUP)]
        for j in range(PEER_SEL):
            c = j % COMBINE_CHAINS
            for g in range(COMBINE_GROUP):
                w = jnp.broadcast_to(splat[g, j:j + 1, :], (ROW_TILES, LANES))
                acc[g][c] = acc[g][c] + w * _table_tile(tab, offs[g](j))
        for g in range(COMBINE_GROUP):
            o_ref[t0 + g] = sum(acc[g])
            splat[g] = nxt[g]
        return carry

    lax.fori_loop(0, rows // COMBINE_GROUP, group, 0)
    x = o_ref[...]
    ms = jnp.sum(jnp.sum(x * x, axis=2, keepdims=True), axis=1, keepdims=True) / D_MODEL
    o_ref[...] = x * lax.rsqrt(ms + EPS) * g_ref[...]


def _gather_specs(rows, tab):
    flat = pl.BlockSpec((rows * PEER_SEL,), lambda i: (i,), memory_space=pltpu.SMEM)
    tiles = pl.BlockSpec((rows, ROW_TILES, LANES), lambda i: (i, 0, 0))
    sel = pl.BlockSpec((rows, PEER_SEL), lambda i: (i, 0))
    table = pl.BlockSpec(tab.shape, lambda i: (0, 0), pipeline_mode=pl.Buffered(1))
    return flat, tiles, sel, table


def _act(off_flat, h3, gate, tab):
    T = h3.shape[0]
    rows = min(GATHER_ROWS, T)
    flat, tiles, sel, table = _gather_specs(rows, tab)
    prod = pltpu.VMEM((ACT_GROUP, PEER_SEL * PACKED_TILES, LANES), jnp.float32)
    return pl.pallas_call(
        _act_kernel,
        grid=(T // rows,),
        in_specs=[flat, tiles, sel, table],
        out_specs=sel,
        out_shape=jax.ShapeDtypeStruct((T, PEER_SEL), jnp.float32),
        scratch_shapes=[prod, pltpu.VMEM((rows, PEER_SEL), jnp.float32)],
        compiler_params=pltpu.CompilerParams(dimension_semantics=("arbitrary",),
                                             vmem_limit_bytes=VMEM_LIMIT),
        name="act",
    )(off_flat, h3, gate, tab)


def _combine(off_flat, w, x3, g3, tab):
    T = x3.shape[0]
    rows = min(GATHER_ROWS, T)
    flat, tiles, sel, table = _gather_specs(rows, tab)
    return pl.pallas_call(
        _combine_kernel,
        grid=(T // rows,),
        in_specs=[flat, sel, tiles, pl.BlockSpec((1, ROW_TILES, LANES), lambda i: (0, 0, 0)), table],
        out_specs=tiles,
        out_shape=jax.ShapeDtypeStruct(x3.shape, jnp.float32),
        scratch_shapes=[pltpu.VMEM((COMBINE_GROUP, PEER_SEL, LANES), jnp.float32)],
        compiler_params=pltpu.CompilerParams(dimension_semantics=("arbitrary",),
                                             vmem_limit_bytes=VMEM_LIMIT),
        name="combine",
    )(off_flat, w, x3, g3, tab)


def kernel(x, mix_norm_g, w_in, pool_w, pool_scale, w_branch_a, conv_w, conv_b, conv_ln_g, conv_ln_b, w_branch_b, gate_b, w_out, ffn_norm_g, peer_w_q, peer_sub_keys, peer_u, peer_v, final_norm_g):
    B, S, D = x.shape
    T = B * S
    assert mix_norm_g.shape[0] == 1, "the combine kernel fuses the final RMSNorm, so exactly one layer"
    bf = lambda w: w[0].astype(jnp.bfloat16)
    x1, h = _mixer(x, mix_norm_g[0], bf(w_in), bf(pool_w), pool_scale[0], bf(w_branch_a), conv_w[0], conv_b[0],
                   conv_ln_g[0], conv_ln_b[0], bf(w_branch_b), gate_b[0], bf(w_out), ffn_norm_g[0])
    h = h.reshape(T, D)
    off, gate = _route(h, bf(peer_w_q), peer_sub_keys[0])
    off_flat = off.reshape(T * PEER_SEL)
    w = _act(off_flat, h.reshape(T, ROW_TILES, LANES), gate, _pack_table(peer_u[0]))
    out = _combine(off_flat, w, x1.reshape(T, ROW_TILES, LANES),
                   final_norm_g.reshape(1, ROW_TILES, LANES), _pack_table(peer_v[0]))
    return out.reshape(B, S, D)
```

```python
import jax
import jax.numpy as jnp
from jax import lax
from jax.experimental import pallas as pl
from jax.experimental.pallas import tpu as pltpu

D_MODEL = 1024
POOL_WIDTH = 512
POOL_GROUPS = 4
POOL_GROUP_DIM = 128
POOL_WINDOWS = (2, 4, 8, 16)
CONV_WIDTH = 512
CONV_KERNEL = 31
PEER_HEADS = 8
PEER_N_KEYS = 128
PEER_TOPK = 16
PEER_HALF = 128
PEER_SEL = PEER_HEADS * PEER_TOPK
EPS = 1e-6

LANES = 128
SUBLANES = 8
ROW_TILES = D_MODEL // LANES
PACKED_TILES = ROW_TILES // 2
HALO = 32
VMEM_LIMIT = 56 * 1024 * 1024

MIX_ROWS = 256
CONV_CHUNK = 32
ROUTE_ROWS = 128
GATHER_ROWS = 256
ACT_GROUP = 4
OFFSET_WINDOW = 8
PACK_ROWS = 512
COMBINE_GROUP = 2
COMBINE_CHAINS = 2

_NT = (((1,), (1,)), ((), ()))


def _rms(x, g):
    return x * lax.rsqrt(jnp.mean(x * x, axis=-1, keepdims=True) + EPS) * g


def _bdot(a, b):
    return jnp.dot(a.astype(jnp.bfloat16), b, preferred_element_type=jnp.float32)


def _mixer_kernel(x_ref, g1_ref, win_ref, poolw_ref, pscale_ref, wa_ref, convw_ref, convb_ref,
                  lng_ref, lnb_ref, wb_ref, gateb_ref, wout_ref, g2_ref,
                  x1_ref, h_ref, zbuf, ubuf, ybuf, ushift):
    s = pl.program_id(1)
    rows = x_ref.shape[0]

    @pl.when(s == 0)
    def _():
        zbuf[0:HALO, :] = jnp.zeros((HALO, POOL_WIDTH), jnp.float32)
        ubuf[0:HALO, :] = jnp.zeros((HALO, CONV_WIDTH), jnp.float32)

    x = x_ref[...]
    proj = _bdot(_rms(x, g1_ref[...]), win_ref[...])
    za = proj[:, :POOL_WIDTH]
    zb = proj[:, POOL_WIDTH:POOL_WIDTH + 2 * CONV_WIDTH]
    zg = proj[:, POOL_WIDTH + 2 * CONV_WIDTH:]

    zbuf[HALO:HALO + rows, :] = za
    pos = (s * rows + lax.broadcasted_iota(jnp.int32, (rows, 1), 0)).astype(jnp.float32)
    pa = []
    for gi, w in enumerate(POOL_WINDOWS):
        cols = slice(gi * POOL_GROUP_DIM, (gi + 1) * POOL_GROUP_DIM)
        zcur = za[:, cols]
        acc = zcur
        for i in range(1, w):
            acc = acc + zbuf[HALO - i:HALO - i + rows, cols]
        pooled = acc / jnp.minimum(pos + 1.0, float(w))
        pa.append(_bdot(pooled - zcur, poolw_ref[gi]))
    pa = jnp.concatenate(pa, axis=1) * pscale_ref[...]
    ya = _bdot(pa, wa_ref[...])

    ubuf[HALO:HALO + rows, :] = zb[:, :CONV_WIDTH] * jax.nn.sigmoid(zb[:, CONV_WIDTH:])
    span = rows + HALO - SUBLANES
    for r in range(1, SUBLANES):
        ushift[r - 1, 0:span, :] = ubuf[r:r + span, :]
    for c in range(rows // CONV_CHUNK):
        acc = jnp.broadcast_to(convb_ref[...], (CONV_CHUNK, CONV_WIDTH))
        for k in range(CONV_KERNEL):
            start = c * CONV_CHUNK + HALO - (CONV_KERNEL - 1) + k
            r, base = start % SUBLANES, start - start % SUBLANES
            taps = ubuf if r == 0 else ushift.at[r - 1]
            acc = acc + convw_ref[k:k + 1, :] * taps[base:base + CONV_CHUNK, :]
        ybuf[c * CONV_CHUNK:(c + 1) * CONV_CHUNK, :] = acc
    y = ybuf[...]
    mu = jnp.mean(y, axis=-1, keepdims=True)
    yc = y - mu
    var = jnp.mean(yc * yc, axis=-1, keepdims=True)
    yn = yc * lax.rsqrt(var + EPS) * lng_ref[...] + lnb_ref[...]
    yb = _bdot(yn * jax.nn.sigmoid(yn), wb_ref[...])

    gates = jax.nn.sigmoid(zg + gateb_ref[...])
    merged = gates[:, :D_MODEL] * ya + gates[:, D_MODEL:] * yb
    x1 = x + _bdot(merged, wout_ref[...])
    x1_ref[...] = x1
    h_ref[...] = _rms(x1, g2_ref[...])

    zbuf[0:HALO, :] = zbuf[rows:rows + HALO, :]
    ubuf[0:HALO, :] = ubuf[rows:rows + HALO, :]


def _mixer(x, g1, w_in, pool_w, pool_scale, w_a, conv_w, conv_b, ln_g, ln_b, w_b, gate_b, w_out, g2):
    B, S, D = x.shape
    rows = min(MIX_ROWS, S)
    const = lambda shape: pl.BlockSpec(shape, lambda b, s: (0,) * len(shape))
    tok = pl.BlockSpec((None, rows, D), lambda b, s: (b, s, 0))
    return pl.pallas_call(
        _mixer_kernel,
        grid=(B, S // rows),
        in_specs=[tok, const((1, D)), const(w_in.shape), const(pool_w.shape), const((1, POOL_WIDTH)),
                  const(w_a.shape), const(conv_w.shape), const((1, CONV_WIDTH)), const((1, CONV_WIDTH)),
                  const((1, CONV_WIDTH)), const(w_b.shape), const((1, 2 * D)), const(w_out.shape),
                  const((1, D))],
        out_specs=[tok, tok],
        out_shape=[jax.ShapeDtypeStruct((B, S, D), jnp.float32)] * 2,
        scratch_shapes=[pltpu.VMEM((HALO + rows, POOL_WIDTH), jnp.float32),
                        pltpu.VMEM((HALO + rows, CONV_WIDTH), jnp.float32),
                        pltpu.VMEM((rows, CONV_WIDTH), jnp.float32),
                        pltpu.VMEM((SUBLANES - 1, HALO + rows, CONV_WIDTH), jnp.float32)],
        compiler_params=pltpu.CompilerParams(dimension_semantics=("arbitrary", "arbitrary"),
                                             vmem_limit_bytes=VMEM_LIMIT),
        name="mixer",
    )(x, g1.reshape(1, D), w_in, pool_w, pool_scale.reshape(1, -1), w_a, conv_w, conv_b.reshape(1, -1),
      ln_g.reshape(1, -1), ln_b.reshape(1, -1), w_b, gate_b.reshape(1, -1), w_out, g2.reshape(1, D))


def _topk_keys(s, k):
    n, tokens = s.shape
    tiles = n // SUBLANES
    assert k == tiles
    row = lax.broadcasted_iota(jnp.int32, (SUBLANES, tokens), 0).astype(jnp.float32)
    vals = [s[v * SUBLANES:(v + 1) * SUBLANES, :] for v in range(tiles)]
    idx = [row + float(v * SUBLANES) for v in range(tiles)]
    for rnd in range(tiles):
        for p in range(rnd % 2, tiles - 1, 2):
            swap = vals[p] < vals[p + 1]
            vals[p], vals[p + 1] = jnp.where(swap, vals[p + 1], vals[p]), jnp.where(swap, vals[p], vals[p + 1])
            idx[p], idx[p + 1] = jnp.where(swap, idx[p + 1], idx[p]), jnp.where(swap, idx[p], idx[p + 1])
    out_v, out_i = [], []
    for r in range(k):
        m = jnp.max(vals[0], axis=0, keepdims=True)
        i = jnp.min(jnp.where(vals[0] == m, idx[0], float(n)), axis=0, keepdims=True)
        hit = idx[0] == i
        out_v.append(m)
        out_i.append(i)
        for lvl in range(k - r - 1):
            vals[lvl] = jnp.where(hit, vals[lvl + 1], vals[lvl])
            idx[lvl] = jnp.where(hit, idx[lvl + 1], idx[lvl])
    return jnp.concatenate(out_v, axis=0), jnp.concatenate(out_i, axis=0)


def _topk_pairs(va, vb, ia, ib):
    K, tokens = va.shape
    assert K == 2 * SUBLANES
    row = lax.broadcasted_iota(jnp.int32, (SUBLANES, tokens), 0).astype(jnp.float32)
    lo_v, lo_e = va[:SUBLANES, :], ia[:SUBLANES, :] * PEER_N_KEYS
    vals, flat, expert = [], [], []
    for j in range(K):
        v = lo_v + vb[j:j + 1, :]
        limit = K // (j + 1)
        vals.append(v if limit >= SUBLANES else jnp.where(row < limit, v, -jnp.inf))
        flat.append(row * K + float(j))
        expert.append(lo_e + ib[j:j + 1, :])
    hi_v = va[SUBLANES:, :] + vb[0:1, :]
    hi_flat = (row + SUBLANES) * K
    hi_e = ia[SUBLANES:, :] * PEER_N_KEYS + ib[0:1, :]
    best, picked = [], []
    for r in range(K):
        m = jnp.max(jnp.maximum(vals[0], hi_v), axis=0, keepdims=True)
        f = jnp.min(jnp.minimum(jnp.where(vals[0] == m, flat[0], float(K * K)),
                                jnp.where(hi_v == m, hi_flat, float(K * K))), axis=0, keepdims=True)
        hit_lo, hit_hi = flat[0] == f, hi_flat == f
        best.append(m)
        picked.append(jnp.max(jnp.maximum(jnp.where(hit_lo, expert[0], -1.0), jnp.where(hit_hi, hi_e, -1.0)),
                              axis=0, keepdims=True))
        for lvl in range(K - r - 1):
            vals[lvl] = jnp.where(hit_lo, vals[lvl + 1], vals[lvl])
            flat[lvl] = jnp.where(hit_lo, flat[lvl + 1], flat[lvl])
            expert[lvl] = jnp.where(hit_lo, expert[lvl + 1], expert[lvl])
        hi_v = jnp.where(hit_hi, -jnp.inf, hi_v)
    return jnp.concatenate(best, axis=0), jnp.concatenate(picked, axis=0)


def _route_kernel(h_ref, wq_ref, sk_ref, off_ref, gate_ref):
    q = _bdot(h_ref[...], wq_ref[...])
    experts, gates = [], []
    for hd in range(PEER_HEADS):
        vals, ids = [], []
        for p in range(2):
            c0 = (hd * 2 + p) * PEER_HALF
            s_t = lax.dot_general(sk_ref[hd, p], q[:, c0:c0 + PEER_HALF], _NT,
                                  preferred_element_type=jnp.float32)
            v, i = _topk_keys(s_t, PEER_TOPK)
            vals.append(v)
            ids.append(i)
        best, expert = _topk_pairs(vals[0], vals[1], ids[0], ids[1])
        experts.append(expert * PACKED_TILES)
        e = jnp.exp(best - best[0:1, :])
        gates.append(e / jnp.sum(e, axis=0, keepdims=True))
    off_ref[...] = jnp.concatenate(experts, axis=0).T.astype(jnp.int32)
    gate_ref[...] = jnp.concatenate(gates, axis=0).T


def _route(h, w_q, sub_keys):
    T, D = h.shape
    rows = min(ROUTE_ROWS, T)
    tok = lambda width: pl.BlockSpec((rows, width), lambda i: (i, 0))
    return pl.pallas_call(
        _route_kernel,
        grid=(T // rows,),
        in_specs=[tok(D), pl.BlockSpec(w_q.shape, lambda i: (0, 0)),
                  pl.BlockSpec(sub_keys.shape, lambda i: (0, 0, 0, 0))],
        out_specs=[tok(PEER_SEL), tok(PEER_SEL)],
        out_shape=[jax.ShapeDtypeStruct((T, PEER_SEL), jnp.int32),
                   jax.ShapeDtypeStruct((T, PEER_SEL), jnp.float32)],
        compiler_params=pltpu.CompilerParams(dimension_semantics=("arbitrary",),
                                             vmem_limit_bytes=VMEM_LIMIT),
        name="route",
    )(h, w_q, sub_keys)


def _pack_kernel(w_ref, o_ref):
    def bf16_bits(v):
        return lax.bitcast_convert_type(v.astype(jnp.bfloat16).astype(jnp.float32), jnp.uint32)

    for s in range(PACKED_TILES):
        lo = bf16_bits(w_ref[:, (2 * s) * LANES:(2 * s + 1) * LANES])
        hi = bf16_bits(w_ref[:, (2 * s + 1) * LANES:(2 * s + 2) * LANES])
        o_ref[:, s * LANES:(s + 1) * LANES] = hi | (lo >> 16)


def _pack_table(w):
    experts, D = w.shape
    words = pl.pallas_call(
        _pack_kernel,
        grid=(experts // PACK_ROWS,),
        in_specs=[pl.BlockSpec((PACK_ROWS, D), lambda i: (i, 0))],
        out_specs=pl.BlockSpec((PACK_ROWS, D // 2), lambda i: (i, 0)),
        out_shape=jax.ShapeDtypeStruct((experts, D // 2), jnp.uint32),
        compiler_params=pltpu.CompilerParams(dimension_semantics=("arbitrary",)),
        name="pack",
    )(w)
    return words.reshape(experts * PACKED_TILES, LANES)


def _offset_reader(off_ref, t):
    windows = [off_ref.at[pl.ds(t * PEER_SEL + s, OFFSET_WINDOW)] for s in range(0, PEER_SEL, OFFSET_WINDOW)]
    return lambda j: windows[j // OFFSET_WINDOW][j % OFFSET_WINDOW]


def _table_tile(tab, offset):
    words = tab[pl.ds(pl.multiple_of(offset, PACKED_TILES), PACKED_TILES), :]
    return pltpu.bitcast(words, jnp.bfloat16).astype(jnp.float32)


def _act_kernel(off_ref, h_ref, gate_ref, tab, w_ref, prod, act):
    rows = h_ref.shape[0]

    def products(t0):
        hs = [h_ref[t0 + g] for g in range(ACT_GROUP)]
        offs = [_offset_reader(off_ref, t0 + g) for g in range(ACT_GROUP)]
        for j in range(PEER_SEL):
            for g in range(ACT_GROUP):
                p = _table_tile(tab, offs[g](j)) * hs[g]
                prod[g, j * PACKED_TILES:(j + 1) * PACKED_TILES, :] = p[:PACKED_TILES] + p[PACKED_TILES:]

    def reduce(t0):
        for g in range(ACT_GROUP):
            folded = prod[g, pl.ds(0, PEER_SEL, stride=PACKED_TILES), :]
            for r in range(1, PACKED_TILES):
                folded = folded + prod[g, pl.ds(r, PEER_SEL, stride=PACKED_TILES), :]
            act[pl.ds(t0 + g, 1), :] = jnp.sum(folded.T, axis=0, keepdims=True)

    prod[...] = jnp.zeros(prod.shape, jnp.float32)

    def group(i, carry):
        reduce(jnp.maximum(i - 1, 0) * ACT_GROUP)
        products(i * ACT_GROUP)
        return carry

    lax.fori_loop(0, rows // ACT_GROUP, group, 0)
    reduce(rows - ACT_GROUP)
    w_ref[...] = gate_ref[...] * jax.nn.gelu(act[...])


def _combine_kernel(off_ref, w_ref, x_ref, g_ref, tab, o_ref, splat):
    rows = x_ref.shape[0]

    def spread(t):
        return jnp.broadcast_to(w_ref[pl.ds(t, 1), :], (LANES, PEER_SEL)).T

    for g in range(COMBINE_GROUP):
        splat[g] = spread(g)

    def group(i, carry):
        t0 = i * COMBINE_GROUP
        nxt = [spread(jnp.minimum(t0 + COMBINE_GROUP + g, rows - 1)) for g in range(COMBINE_GROUP)]
        offs = [_offset_reader(off_ref, t0 + g) for g in range(COMBINE_GROUP)]
        acc = [[x_ref[t0 + g]] + [jnp.zeros((ROW_TILES, LANES), jnp.float32)] * (COMBINE_CHAINS - 1)
               for g in range(COMBINE_GROUP)]
        for j in range(PEER_SEL):
            c = j % COMBINE_CHAINS
            for g in range(COMBINE_GROUP):
                w = jnp.broadcast_to(splat[g, j:j + 1, :], (ROW_TILES, LANES))
                acc[g][c] = acc[g][c] + w * _table_tile(tab, offs[g](j))
        for g in range(COMBINE_GROUP):
            o_ref[t0 + g] = sum(acc[g])
            splat[g] = nxt[g]
        return carry

    lax.fori_loop(0, rows // COMBINE_GROUP, group, 0)
    x = o_ref[...]
    ms = jnp.sum(jnp.sum(x * x, axis=2, keepdims=True), axis=1, keepdims=True) / D_MODEL
    o_ref[...] = x * lax.rsqrt(ms + EPS) * g_ref[...]


def _gather_specs(rows, tab):
    flat = pl.BlockSpec((rows * PEER_SEL,), lambda i: (i,), memory_space=pltpu.SMEM)
    tiles = pl.BlockSpec((rows, ROW_TILES, LANES), lambda i: (i, 0, 0))
    sel = pl.BlockSpec((rows, PEER_SEL), lambda i: (i, 0))
    table = pl.BlockSpec(tab.shape, lambda i: (0, 0), pipeline_mode=pl.Buffered(1))
    return flat, tiles, sel, table


def _act(off_flat, h3, gate, tab):
    T = h3.shape[0]
    rows = min(GATHER_ROWS, T)
    flat, tiles, sel, table = _gather_specs(rows, tab)
    prod = pltpu.VMEM((ACT_GROUP, PEER_SEL * PACKED_TILES, LANES), jnp.float32)
    return pl.pallas_call(
        _act_kernel,
        grid=(T // rows,),
        in_specs=[flat, tiles, sel, table],
        out_specs=sel,
        out_shape=jax.ShapeDtypeStruct((T, PEER_SEL), jnp.float32),
        scratch_shapes=[prod, pltpu.VMEM((rows, PEER_SEL), jnp.float32)],
        compiler_params=pltpu.CompilerParams(dimension_semantics=("arbitrary",),
                                             vmem_limit_bytes=VMEM_LIMIT),
        name="act",
    )(off_flat, h3, gate, tab)


def _combine(off_flat, w, x3, g3, tab):
    T = x3.shape[0]
    rows = min(GATHER_ROWS, T)
    flat, tiles, sel, table = _gather_specs(rows, tab)
    return pl.pallas_call(
        _combine_kernel,
        grid=(T // rows,),
        in_specs=[flat, sel, tiles, pl.BlockSpec((1, ROW_TILES, LANES), lambda i: (0, 0, 0)), table],
        out_specs=tiles,
        out_shape=jax.ShapeDtypeStruct(x3.shape, jnp.float32),
        scratch_shapes=[pltpu.VMEM((COMBINE_GROUP, PEER_SEL, LANES), jnp.float32)],
        compiler_params=pltpu.CompilerParams(dimension_semantics=("arbitrary",),
                                             vmem_limit_bytes=VMEM_LIMIT),
        name="combine",
    )(off_flat, w, x3, g3, tab)


def kernel(x, mix_norm_g, w_in, pool_w, pool_scale, w_branch_a, conv_w, conv_b, conv_ln_g, conv_ln_b, w_branch_b, gate_b, w_out, ffn_norm_g, peer_w_q, peer_sub_keys, peer_u, peer_v, final_norm_g):
    B, S, D = x.shape
    T = B * S
    assert mix_norm_g.shape[0] == 1, "the combine kernel fuses the final RMSNorm, so exactly one layer"
    bf = lambda w: w[0].astype(jnp.bfloat16)
    x1, h = _mixer(x, mix_norm_g[0], bf(w_in), bf(pool_w), pool_scale[0], bf(w_branch_a), conv_w[0], conv_b[0],
                   conv_ln_g[0], conv_ln_b[0], bf(w_branch_b), gate_b[0], bf(w_out), ffn_norm_g[0])
    h = h.reshape(T, D)
    off, gate = _route(h, bf(peer_w_q), peer_sub_keys[0])
    off_flat = off.reshape(T * PEER_SEL)
    w = _act(off_flat, h.reshape(T, ROW_TILES, LANES), gate, _pack_table(peer_u[0]))
    out = _combine(off_flat, w, x1.reshape(T, ROW_TILES, LANES),
                   final_norm_g.reshape(1, ROW_TILES, LANES), _pack_table(peer_v[0]))
    return out.reshape(B, S, D)
```

```python
import jax
import jax.numpy as jnp
from jax import lax
from jax.experimental import pallas as pl
from jax.experimental.pallas import tpu as pltpu

D_MODEL = 1024
POOL_WIDTH = 512
POOL_GROUPS = 4
POOL_GROUP_DIM = 128
POOL_WINDOWS = (2, 4, 8, 16)
CONV_WIDTH = 512
CONV_KERNEL = 31
PEER_HEADS = 8
PEER_N_KEYS = 128
PEER_TOPK = 16
PEER_HALF = 128
PEER_SEL = PEER_HEADS * PEER_TOPK
EPS = 1e-6

LANES = 128
SUBLANES = 8
ROW_TILES = D_MODEL // LANES
PACKED_TILES = ROW_TILES // 2
HALO = 32
VMEM_LIMIT = 56 * 1024 * 1024

MIX_ROWS = 256
CONV_CHUNK = 32
ROUTE_ROWS = 128
KEY_COLUMN_DEPTH = 8
GATHER_ROWS = 256
ACT_GROUP = 4
OFFSET_WINDOW = 8
PACK_ROWS = 512
COMBINE_GROUP = 2
COMBINE_CHAINS = 2

_NT = (((1,), (1,)), ((), ()))


def _rms(x, g):
    return x * lax.rsqrt(jnp.mean(x * x, axis=-1, keepdims=True) + EPS) * g


def _bdot(a, b):
    return jnp.dot(a.astype(jnp.bfloat16), b, preferred_element_type=jnp.float32)


def _mixer_kernel(x_ref, g1_ref, win_ref, poolw_ref, pscale_ref, wa_ref, convw_ref, convb_ref,
                  lng_ref, lnb_ref, wb_ref, gateb_ref, wout_ref, g2_ref,
                  x1_ref, h_ref, zbuf, ubuf, ybuf, ushift):
    s = pl.program_id(1)
    rows = x_ref.shape[0]

    @pl.when(s == 0)
    def _():
        zbuf[0:HALO, :] = jnp.zeros((HALO, POOL_WIDTH), jnp.float32)
        ubuf[0:HALO, :] = jnp.zeros((HALO, CONV_WIDTH), jnp.float32)

    x = x_ref[...]
    proj = _bdot(_rms(x, g1_ref[...]), win_ref[...])
    za = proj[:, :POOL_WIDTH]
    zb = proj[:, POOL_WIDTH:POOL_WIDTH + 2 * CONV_WIDTH]
    zg = proj[:, POOL_WIDTH + 2 * CONV_WIDTH:]

    zbuf[HALO:HALO + rows, :] = za
    pos = (s * rows + lax.broadcasted_iota(jnp.int32, (rows, 1), 0)).astype(jnp.float32)
    pa = []
    for gi, w in enumerate(POOL_WINDOWS):
        cols = slice(gi * POOL_GROUP_DIM, (gi + 1) * POOL_GROUP_DIM)
        zcur = za[:, cols]
        acc = zcur
        for i in range(1, w):
            acc = acc + zbuf[HALO - i:HALO - i + rows, cols]
        pooled = acc / jnp.minimum(pos + 1.0, float(w))
        pa.append(_bdot(pooled - zcur, poolw_ref[gi]))
    pa = jnp.concatenate(pa, axis=1) * pscale_ref[...]
    ya = _bdot(pa, wa_ref[...])

    ubuf[HALO:HALO + rows, :] = zb[:, :CONV_WIDTH] * jax.nn.sigmoid(zb[:, CONV_WIDTH:])
    span = rows + HALO - SUBLANES
    for r in range(1, SUBLANES):
        ushift[r - 1, 0:span, :] = ubuf[r:r + span, :]
    for c in range(rows // CONV_CHUNK):
        acc = jnp.broadcast_to(convb_ref[...], (CONV_CHUNK, CONV_WIDTH))
        for k in range(CONV_KERNEL):
            start = c * CONV_CHUNK + HALO - (CONV_KERNEL - 1) + k
            r, base = start % SUBLANES, start - start % SUBLANES
            taps = ubuf if r == 0 else ushift.at[r - 1]
            acc = acc + convw_ref[k:k + 1, :] * taps[base:base + CONV_CHUNK, :]
        ybuf[c * CONV_CHUNK:(c + 1) * CONV_CHUNK, :] = acc
    y = ybuf[...]
    mu = jnp.mean(y, axis=-1, keepdims=True)
    yc = y - mu
    var = jnp.mean(yc * yc, axis=-1, keepdims=True)
    yn = yc * lax.rsqrt(var + EPS) * lng_ref[...] + lnb_ref[...]
    yb = _bdot(yn * jax.nn.sigmoid(yn), wb_ref[...])

    gates = jax.nn.sigmoid(zg + gateb_ref[...])
    merged = gates[:, :D_MODEL] * ya + gates[:, D_MODEL:] * yb
    x1 = x + _bdot(merged, wout_ref[...])
    x1_ref[...] = x1
    h_ref[...] = _rms(x1, g2_ref[...])

    zbuf[0:HALO, :] = zbuf[rows:rows + HALO, :]
    ubuf[0:HALO, :] = ubuf[rows:rows + HALO, :]


def _mixer(x, g1, w_in, pool_w, pool_scale, w_a, conv_w, conv_b, ln_g, ln_b, w_b, gate_b, w_out, g2):
    B, S, D = x.shape
    rows = min(MIX_ROWS, S)
    const = lambda shape: pl.BlockSpec(shape, lambda b, s: (0,) * len(shape))
    tok = pl.BlockSpec((None, rows, D), lambda b, s: (b, s, 0))
    return pl.pallas_call(
        _mixer_kernel,
        grid=(B, S // rows),
        in_specs=[tok, const((1, D)), const(w_in.shape), const(pool_w.shape), const((1, POOL_WIDTH)),
                  const(w_a.shape), const(conv_w.shape), const((1, CONV_WIDTH)), const((1, CONV_WIDTH)),
                  const((1, CONV_WIDTH)), const(w_b.shape), const((1, 2 * D)), const(w_out.shape),
                  const((1, D))],
        out_specs=[tok, tok],
        out_shape=[jax.ShapeDtypeStruct((B, S, D), jnp.float32)] * 2,
        scratch_shapes=[pltpu.VMEM((HALO + rows, POOL_WIDTH), jnp.float32),
                        pltpu.VMEM((HALO + rows, CONV_WIDTH), jnp.float32),
                        pltpu.VMEM((rows, CONV_WIDTH), jnp.float32),
                        pltpu.VMEM((SUBLANES - 1, HALO + rows, CONV_WIDTH), jnp.float32)],
        compiler_params=pltpu.CompilerParams(dimension_semantics=("arbitrary", "arbitrary"),
                                             vmem_limit_bytes=VMEM_LIMIT),
        name="mixer",
    )(x, g1.reshape(1, D), w_in, pool_w, pool_scale.reshape(1, -1), w_a, conv_w, conv_b.reshape(1, -1),
      ln_g.reshape(1, -1), ln_b.reshape(1, -1), w_b, gate_b.reshape(1, -1), w_out, g2.reshape(1, D))


def _topk_keys(s, k):
    n, tokens = s.shape
    tiles, depth = n // SUBLANES, KEY_COLUMN_DEPTH
    assert k == tiles and tiles % depth == 0
    groups = tiles // depth
    row = lax.broadcasted_iota(jnp.int32, (SUBLANES, tokens), 0).astype(jnp.float32)
    tile = lambda v: s[v * SUBLANES:(v + 1) * SUBLANES, :]
    vals = [[tile(g * depth + l) for l in range(depth)] for g in range(groups)]
    idx = [[row + float((g * depth + l) * SUBLANES) for l in range(depth)] for g in range(groups)]
    for v, ix in zip(vals, idx):
        for rnd in range(depth):
            for p in range(rnd % 2, depth - 1, 2):
                swap = v[p] < v[p + 1]
                v[p], v[p + 1] = jnp.where(swap, v[p + 1], v[p]), jnp.where(swap, v[p], v[p + 1])
                ix[p], ix[p + 1] = jnp.where(swap, ix[p + 1], ix[p]), jnp.where(swap, ix[p], ix[p + 1])
    out_v, out_i = [], []
    for r in range(k):
        head = vals[0][0]
        for v in vals[1:]:
            head = jnp.maximum(head, v[0])
        m = jnp.max(head, axis=0, keepdims=True)
        first = jnp.where(vals[0][0] == m, idx[0][0], float(n))
        for v, ix in zip(vals[1:], idx[1:]):
            first = jnp.minimum(first, jnp.where(v[0] == m, ix[0], float(n)))
        i = jnp.min(first, axis=0, keepdims=True)
        out_v.append(m)
        out_i.append(i)
        live = min(depth, k - r)
        for v, ix in zip(vals, idx):
            hit = ix[0] == i
            for lvl in range(live - 1):
                v[lvl] = jnp.where(hit, v[lvl + 1], v[lvl])
                ix[lvl] = jnp.where(hit, ix[lvl + 1], ix[lvl])
            if live == depth:
                v[depth - 1] = jnp.where(hit, -jnp.inf, v[depth - 1])
    return jnp.concatenate(out_v, axis=0), jnp.concatenate(out_i, axis=0)


def _topk_pairs(va, vb, ia, ib):
    K, tokens = va.shape
    assert K == 2 * SUBLANES
    row = lax.broadcasted_iota(jnp.int32, (SUBLANES, tokens), 0).astype(jnp.float32)
    lo_v, lo_e = va[:SUBLANES, :], ia[:SUBLANES, :] * PEER_N_KEYS
    vals, flat, expert = [], [], []
    for j in range(K):
        v = lo_v + vb[j:j + 1, :]
        limit = K // (j + 1)
        vals.append(v if limit >= SUBLANES else jnp.where(row < limit, v, -jnp.inf))
        flat.append(row * K + float(j))
        expert.append(lo_e + ib[j:j + 1, :])
    hi_v = va[SUBLANES:, :] + vb[0:1, :]
    hi_flat = (row + SUBLANES) * K
    hi_e = ia[SUBLANES:, :] * PEER_N_KEYS + ib[0:1, :]
    best, picked = [], []
    for r in range(K):
        m = jnp.max(jnp.maximum(vals[0], hi_v), axis=0, keepdims=True)
        f = jnp.min(jnp.minimum(jnp.where(vals[0] == m, flat[0], float(K * K)),
                                jnp.where(hi_v == m, hi_flat, float(K * K))), axis=0, keepdims=True)
        hit_lo, hit_hi = flat[0] == f, hi_flat == f
        best.append(m)
        picked.append(jnp.max(jnp.maximum(jnp.where(hit_lo, expert[0], -1.0), jnp.where(hit_hi, hi_e, -1.0)),
                              axis=0, keepdims=True))
        for lvl in range(K - r - 1):
            vals[lvl] = jnp.where(hit_lo, vals[lvl + 1], vals[lvl])
            flat[lvl] = jnp.where(hit_lo, flat[lvl + 1], flat[lvl])
            expert[lvl] = jnp.where(hit_lo, expert[lvl + 1], expert[lvl])
        hi_v = jnp.where(hit_hi, -jnp.inf, hi_v)
    return jnp.concatenate(best, axis=0), jnp.concatenate(picked, axis=0)


def _route_kernel(h_ref, wq_ref, sk_ref, off_ref, gate_ref):
    q = _bdot(h_ref[...], wq_ref[...])
    experts, gates = [], []
    for hd in range(PEER_HEADS):
        vals, ids = [], []
        for p in range(2):
            c0 = (hd * 2 + p) * PEER_HALF
            s_t = lax.dot_general(sk_ref[hd, p], q[:, c0:c0 + PEER_HALF], _NT,
                                  preferred_element_type=jnp.float32)
            v, i = _topk_keys(s_t, PEER_TOPK)
            vals.append(v)
            ids.append(i)
        best, expert = _topk_pairs(vals[0], vals[1], ids[0], ids[1])
        experts.append(expert * PACKED_TILES)
        e = jnp.exp(best - best[0:1, :])
        gates.append(e / jnp.sum(e, axis=0, keepdims=True))
    off_ref[...] = jnp.concatenate(experts, axis=0).T.astype(jnp.int32)
    gate_ref[...] = jnp.concatenate(gates, axis=0).T


def _route(h, w_q, sub_keys):
    T, D = h.shape
    rows = min(ROUTE_ROWS, T)
    tok = lambda width: pl.BlockSpec((rows, width), lambda i: (i, 0))
    return pl.pallas_call(
        _route_kernel,
        grid=(T // rows,),
        in_specs=[tok(D), pl.BlockSpec(w_q.shape, lambda i: (0, 0)),
                  pl.BlockSpec(sub_keys.shape, lambda i: (0, 0, 0, 0))],
        out_specs=[tok(PEER_SEL), tok(PEER_SEL)],
        out_shape=[jax.ShapeDtypeStruct((T, PEER_SEL), jnp.int32),
                   jax.ShapeDtypeStruct((T, PEER_SEL), jnp.float32)],
        compiler_params=pltpu.CompilerParams(dimension_semantics=("arbitrary",),
                                             vmem_limit_bytes=VMEM_LIMIT),
        name="route",
    )(h, w_q, sub_keys)


def _pack_kernel(w_ref, o_ref):
    def bf16_bits(v):
        return lax.bitcast_convert_type(v.astype(jnp.bfloat16).astype(jnp.float32), jnp.uint32)

    for s in range(PACKED_TILES):
        lo = bf16_bits(w_ref[:, (2 * s) * LANES:(2 * s + 1) * LANES])
        hi = bf16_bits(w_ref[:, (2 * s + 1) * LANES:(2 * s + 2) * LANES])
        o_ref[:, s * LANES:(s + 1) * LANES] = hi | (lo >> 16)


def _pack_table(w):
    experts, D = w.shape
    words = pl.pallas_call(
        _pack_kernel,
        grid=(experts // PACK_ROWS,),
        in_specs=[pl.BlockSpec((PACK_ROWS, D), lambda i: (i, 0))],
        out_specs=pl.BlockSpec((PACK_ROWS, D // 2), lambda i: (i, 0)),
        out_shape=jax.ShapeDtypeStruct((experts, D // 2), jnp.uint32),
        compiler_params=pltpu.CompilerParams(dimension_semantics=("arbitrary",)),
        name="pack",
    )(w)
    return words.reshape(experts * PACKED_TILES, LANES)


def _offset_reader(off_ref, t):
    windows = [off_ref.at[pl.ds(t * PEER_SEL + s, OFFSET_WINDOW)] for s in range(0, PEER_SEL, OFFSET_WINDOW)]
    return lambda j: windows[j // OFFSET_WINDOW][j % OFFSET_WINDOW]


def _table_tile(tab, offset):
    words = tab[pl.ds(pl.multiple_of(offset, PACKED_TILES), PACKED_TILES), :]
    return pltpu.bitcast(words, jnp.bfloat16).astype(jnp.float32)


def _act_kernel(off_ref, h_ref, gate_ref, tab, w_ref, prod, act):
    rows = h_ref.shape[0]

    def products(t0):
        hs = [h_ref[t0 + g] for g in range(ACT_GROUP)]
        offs = [_offset_reader(off_ref, t0 + g) for g in range(ACT_GROUP)]
        for j in range(PEER_SEL):
            for g in range(ACT_GROUP):
                p = _table_tile(tab, offs[g](j)) * hs[g]
                prod[g, j * PACKED_TILES:(j + 1) * PACKED_TILES, :] = p[:PACKED_TILES] + p[PACKED_TILES:]

    def reduce(t0):
        for g in range(ACT_GROUP):
            folded = prod[g, pl.ds(0, PEER_SEL, stride=PACKED_TILES), :]
            for r in range(1, PACKED_TILES):
                folded = folded + prod[g, pl.ds(r, PEER_SEL, stride=PACKED_TILES), :]
            act[pl.ds(t0 + g, 1), :] = jnp.sum(folded.T, axis=0, keepdims=True)

    prod[...] = jnp.zeros(prod.shape, jnp.float32)

    def group(i, carry):
        reduce(jnp.maximum(i - 1, 0) * ACT_GROUP)
        products(i * ACT_GROUP)
        return carry

    lax.fori_loop(0, rows // ACT_GROUP, group, 0)
    reduce(rows - ACT_GROUP)
    w_ref[...] = gate_ref[...] * jax.nn.gelu(act[...])


def _combine_kernel(off_ref, w_ref, x_ref, g_ref, tab, o_ref, splat):
    rows = x_ref.shape[0]

    def spread(t):
        return jnp.broadcast_to(w_ref[pl.ds(t, 1), :], (LANES, PEER_SEL)).T

    for g in range(COMBINE_GROUP):
        splat[g] = spread(g)

    def group(i, carry):
        t0 = i * COMBINE_GROUP
        nxt = [spread(jnp.minimum(t0 + COMBINE_GROUP + g, rows - 1)) for g in range(COMBINE_GROUP)]
        offs = [_offset_reader(off_ref, t0 + g) for g in range(COMBINE_GROUP)]
        acc = [[x_ref[t0 + g]] + [jnp.zeros((ROW_TILES, LANES), jnp.float32)] * (COMBINE_CHAINS - 1)
               for g in range(COMBINE_GROUP)]
        for j in range(PEER_SEL):
            c = j % COMBINE_CHAINS
            for g in range(COMBINE_GROUP):
                w = jnp.broadcast_to(splat[g, j:j + 1, :], (ROW_TILES, LANES))
                acc[g][c] = acc[g][c] + w * _table_tile(tab, offs[g](j))
        for g in range(COMBINE_GROUP):
            o_ref[t0 + g] = sum(acc[g])
            splat[g] = nxt[g]
        return carry

    lax.fori_loop(0, rows // COMBINE_GROUP, group, 0)
    x = o_ref[...]
    ms = jnp.sum(jnp.sum(x * x, axis=2, keepdims=True), axis=1, keepdims=True) / D_MODEL
    o_ref[...] = x * lax.rsqrt(ms + EPS) * g_ref[...]


def _gather_specs(rows, tab):
    flat = pl.BlockSpec((rows * PEER_SEL,), lambda i: (i,), memory_space=pltpu.SMEM)
    tiles = pl.BlockSpec((rows, ROW_TILES, LANES), lambda i: (i, 0, 0))
    sel = pl.BlockSpec((rows, PEER_SEL), lambda i: (i, 0))
    table = pl.BlockSpec(tab.shape, lambda i: (0, 0), pipeline_mode=pl.Buffered(1))
    return flat, tiles, sel, table


def _act(off_flat, h3, gate, tab):
    T = h3.shape[0]
    rows = min(GATHER_ROWS, T)
    flat, tiles, sel, table = _gather_specs(rows, tab)
    prod = pltpu.VMEM((ACT_GROUP, PEER_SEL * PACKED_TILES, LANES), jnp.float32)
    return pl.pallas_call(
        _act_kernel,
        grid=(T // rows,),
        in_specs=[flat, tiles, sel, table],
        out_specs=sel,
        out_shape=jax.ShapeDtypeStruct((T, PEER_SEL), jnp.float32),
        scratch_shapes=[prod, pltpu.VMEM((rows, PEER_SEL), jnp.float32)],
        compiler_params=pltpu.CompilerParams(dimension_semantics=("arbitrary",),
                                             vmem_limit_bytes=VMEM_LIMIT),
        name="act",
    )(off_flat, h3, gate, tab)


def _combine(off_flat, w, x3, g3, tab):
    T = x3.shape[0]
    rows = min(GATHER_ROWS, T)
    flat, tiles, sel, table = _gather_specs(rows, tab)
    return pl.pallas_call(
        _combine_kernel,
        grid=(T // rows,),
        in_specs=[flat, sel, tiles, pl.BlockSpec((1, ROW_TILES, LANES), lambda i: (0, 0, 0)), table],
        out_specs=tiles,
        out_shape=jax.ShapeDtypeStruct(x3.shape, jnp.float32),
        scratch_shapes=[pltpu.VMEM((COMBINE_GROUP, PEER_SEL, LANES), jnp.float32)],
        compiler_params=pltpu.CompilerParams(dimension_semantics=("arbitrary",),
                                             vmem_limit_bytes=VMEM_LIMIT),
        name="combine",
    )(off_flat, w, x3, g3, tab)


def kernel(x, mix_norm_g, w_in, pool_w, pool_scale, w_branch_a, conv_w, conv_b, conv_ln_g, conv_ln_b, w_branch_b, gate_b, w_out, ffn_norm_g, peer_w_q, peer_sub_keys, peer_u, peer_v, final_norm_g):
    B, S, D = x.shape
    T = B * S
    assert mix_norm_g.shape[0] == 1, "the combine kernel fuses the final RMSNorm, so exactly one layer"
    bf = lambda w: w[0].astype(jnp.bfloat16)
    x1, h = _mixer(x, mix_norm_g[0], bf(w_in), bf(pool_w), pool_scale[0], bf(w_branch_a), conv_w[0], conv_b[0],
                   conv_ln_g[0], conv_ln_b[0], bf(w_branch_b), gate_b[0], bf(w_out), ffn_norm_g[0])
    h = h.reshape(T, D)
    off, gate = _route(h, bf(peer_w_q), peer_sub_keys[0])
    off_flat = off.reshape(T * PEER_SEL)
    w = _act(off_flat, h.reshape(T, ROW_TILES, LANES), gate, _pack_table(peer_u[0]))
    out = _combine(off_flat, w, x1.reshape(T, ROW_TILES, LANES),
                   final_norm_g.reshape(1, ROW_TILES, LANES), _pack_table(peer_v[0]))
    return out.reshape(B, S, D)
```

```python
import functools

import jax
import jax.numpy as jnp
from jax import lax
from jax.experimental import pallas as pl
from jax.experimental.pallas import tpu as pltpu

D_MODEL = 1024
POOL_WIDTH = 512
POOL_GROUPS = 4
POOL_GROUP_DIM = 128
POOL_WINDOWS = (2, 4, 8, 16)
CONV_WIDTH = 512
CONV_KERNEL = 31
PEER_HEADS = 8
PEER_N_KEYS = 128
PEER_TOPK = 16
PEER_HALF = 128
PEER_SEL = PEER_HEADS * PEER_TOPK
EPS = 1e-6

LANES = 128
SUBLANES = 8
ROW_TILES = D_MODEL // LANES
PACKED_TILES = ROW_TILES // 2
HALO = 32
VMEM_LIMIT = 56 * 1024 * 1024

MIX_ROWS = 256
CONV_CHUNK = 32
ROUTE_ROWS = 128
KEY_COLUMN_DEPTH = 8
GATHER_ROWS = 256
ACT_GROUP = 4
OFFSET_WINDOW = 8
PACK_ROWS = 512
COMBINE_GROUP = 2
COMBINE_CHAINS = 2

_NT = (((1,), (1,)), ((), ()))


def _rms(x, g):
    return x * lax.rsqrt(jnp.mean(x * x, axis=-1, keepdims=True) + EPS) * g


def _bdot(a, b):
    return jnp.dot(a.astype(jnp.bfloat16), b, preferred_element_type=jnp.float32)


def _mixer_kernel(x_ref, g1_ref, win_ref, poolw_ref, pscale_ref, wa_ref, convw_ref, convb_ref,
                  lng_ref, lnb_ref, wb_ref, gateb_ref, wout_ref, g2_ref, wq_ref, sk_ref,
                  x1_ref, h_ref, off_ref, gate_ref, zbuf, ubuf, ybuf, ushift, hprev, *, seq_steps):
    step = jnp.minimum(pl.program_id(0), pl.num_programs(0) - 2)
    s = step % seq_steps
    rows = x_ref.shape[0]

    @pl.when(pl.program_id(0) == 0)
    def _():
        hprev[...] = jnp.zeros(hprev.shape, jnp.float32)

    @pl.when(s == 0)
    def _():
        zbuf[0:HALO, :] = jnp.zeros((HALO, POOL_WIDTH), jnp.float32)
        ubuf[0:HALO, :] = jnp.zeros((HALO, CONV_WIDTH), jnp.float32)

    for c in range(rows // ROUTE_ROWS):
        part = slice(c * ROUTE_ROWS, (c + 1) * ROUTE_ROWS)
        off_ref[part, :], gate_ref[part, :] = _route_tokens(hprev[part, :], wq_ref, sk_ref)

    x = x_ref[...]
    proj = _bdot(_rms(x, g1_ref[...]), win_ref[...])
    za = proj[:, :POOL_WIDTH]
    zb = proj[:, POOL_WIDTH:POOL_WIDTH + 2 * CONV_WIDTH]
    zg = proj[:, POOL_WIDTH + 2 * CONV_WIDTH:]

    zbuf[HALO:HALO + rows, :] = za
    pos = (s * rows + lax.broadcasted_iota(jnp.int32, (rows, 1), 0)).astype(jnp.float32)
    pa = []
    for gi, w in enumerate(POOL_WINDOWS):
        cols = slice(gi * POOL_GROUP_DIM, (gi + 1) * POOL_GROUP_DIM)
        zcur = za[:, cols]
        acc = zcur
        for i in range(1, w):
            acc = acc + zbuf[HALO - i:HALO - i + rows, cols]
        pooled = acc / jnp.minimum(pos + 1.0, float(w))
        pa.append(_bdot(pooled - zcur, poolw_ref[gi]))
    pa = jnp.concatenate(pa, axis=1) * pscale_ref[...]
    ya = _bdot(pa, wa_ref[...])

    ubuf[HALO:HALO + rows, :] = zb[:, :CONV_WIDTH] * jax.nn.sigmoid(zb[:, CONV_WIDTH:])
    span = rows + HALO - SUBLANES
    for r in range(1, SUBLANES):
        ushift[r - 1, 0:span, :] = ubuf[r:r + span, :]
    for c in range(rows // CONV_CHUNK):
        acc = jnp.broadcast_to(convb_ref[...], (CONV_CHUNK, CONV_WIDTH))
        for k in range(CONV_KERNEL):
            start = c * CONV_CHUNK + HALO - (CONV_KERNEL - 1) + k
            r, base = start % SUBLANES, start - start % SUBLANES
            taps = ubuf if r == 0 else ushift.at[r - 1]
            acc = acc + convw_ref[k:k + 1, :] * taps[base:base + CONV_CHUNK, :]
        ybuf[c * CONV_CHUNK:(c + 1) * CONV_CHUNK, :] = acc
    y = ybuf[...]
    mu = jnp.mean(y, axis=-1, keepdims=True)
    yc = y - mu
    var = jnp.mean(yc * yc, axis=-1, keepdims=True)
    yn = yc * lax.rsqrt(var + EPS) * lng_ref[...] + lnb_ref[...]
    yb = _bdot(yn * jax.nn.sigmoid(yn), wb_ref[...])

    gates = jax.nn.sigmoid(zg + gateb_ref[...])
    merged = gates[:, :D_MODEL] * ya + gates[:, D_MODEL:] * yb
    x1 = x + _bdot(merged, wout_ref[...])
    x1_ref[...] = x1
    h = _rms(x1, g2_ref[...])
    h_ref[...] = h
    hprev[...] = h

    @pl.when(s < seq_steps - 1)
    def _():
        zbuf[0:HALO, :] = zbuf[rows:rows + HALO, :]
        ubuf[0:HALO, :] = ubuf[rows:rows + HALO, :]


def _mixer(x, g1, w_in, pool_w, pool_scale, w_a, conv_w, conv_b, ln_g, ln_b, w_b, gate_b, w_out, g2, w_q, sub_keys):
    B, S, D = x.shape
    rows = min(MIX_ROWS, S)
    seq_steps = S // rows
    steps = B * seq_steps
    const = lambda shape: pl.BlockSpec(shape, lambda i: (0,) * len(shape))

    def block_of(step):
        return step // seq_steps, step % seq_steps, 0

    tok = pl.BlockSpec((None, rows, D), lambda i: block_of(jnp.minimum(i, steps - 1)))
    sel = pl.BlockSpec((None, rows, PEER_SEL), lambda i: block_of(jnp.maximum(i - 1, 0)))
    return pl.pallas_call(
        functools.partial(_mixer_kernel, seq_steps=seq_steps),
        grid=(steps + 1,),
        in_specs=[tok, const((1, D)), const(w_in.shape), const(pool_w.shape), const((1, POOL_WIDTH)),
                  const(w_a.shape), const(conv_w.shape), const((1, CONV_WIDTH)), const((1, CONV_WIDTH)),
                  const((1, CONV_WIDTH)), const(w_b.shape), const((1, 2 * D)), const(w_out.shape),
                  const((1, D)), const(w_q.shape), const(sub_keys.shape)],
        out_specs=[tok, tok, sel, sel],
        out_shape=[jax.ShapeDtypeStruct((B, S, D), jnp.float32)] * 2
        + [jax.ShapeDtypeStruct((B, S, PEER_SEL), jnp.int32), jax.ShapeDtypeStruct((B, S, PEER_SEL), jnp.float32)],
        scratch_shapes=[pltpu.VMEM((HALO + rows, POOL_WIDTH), jnp.float32),
                        pltpu.VMEM((HALO + rows, CONV_WIDTH), jnp.float32),
                        pltpu.VMEM((rows, CONV_WIDTH), jnp.float32),
                        pltpu.VMEM((SUBLANES - 1, HALO + rows, CONV_WIDTH), jnp.float32),
                        pltpu.VMEM((rows, D), jnp.float32)],
        compiler_params=pltpu.CompilerParams(dimension_semantics=("arbitrary",),
                                             vmem_limit_bytes=VMEM_LIMIT),
        name="mixer",
    )(x, g1.reshape(1, D), w_in, pool_w, pool_scale.reshape(1, -1), w_a, conv_w, conv_b.reshape(1, -1),
      ln_g.reshape(1, -1), ln_b.reshape(1, -1), w_b, gate_b.reshape(1, -1), w_out, g2.reshape(1, D), w_q, sub_keys)


def _topk_keys(s, k):
    n, tokens = s.shape
    tiles, depth = n // SUBLANES, KEY_COLUMN_DEPTH
    assert k == tiles and tiles % depth == 0
    groups = tiles // depth
    row = lax.broadcasted_iota(jnp.int32, (SUBLANES, tokens), 0).astype(jnp.float32)
    tile = lambda v: s[v * SUBLANES:(v + 1) * SUBLANES, :]
    vals = [[tile(g * depth + l) for l in range(depth)] for g in range(groups)]
    idx = [[row + float((g * depth + l) * SUBLANES) for l in range(depth)] for g in range(groups)]
    for v, ix in zip(vals, idx):
        for rnd in range(depth):
            for p in range(rnd % 2, depth - 1, 2):
                swap = v[p] < v[p + 1]
                v[p], v[p + 1] = jnp.where(swap, v[p + 1], v[p]), jnp.where(swap, v[p], v[p + 1])
                ix[p], ix[p + 1] = jnp.where(swap, ix[p + 1], ix[p]), jnp.where(swap, ix[p], ix[p + 1])
    out_v, out_i = [], []
    for r in range(k):
        head = vals[0][0]
        for v in vals[1:]:
            head = jnp.maximum(head, v[0])
        m = jnp.max(head, axis=0, keepdims=True)
        first = jnp.where(vals[0][0] == m, idx[0][0], float(n))
        for v, ix in zip(vals[1:], idx[1:]):
            first = jnp.minimum(first, jnp.where(v[0] == m, ix[0], float(n)))
        i = jnp.min(first, axis=0, keepdims=True)
        out_v.append(m)
        out_i.append(i)
        live = min(depth, k - r)
        for v, ix in zip(vals, idx):
            hit = ix[0] == i
            for lvl in range(live - 1):
                v[lvl] = jnp.where(hit, v[lvl + 1], v[lvl])
                ix[lvl] = jnp.where(hit, ix[lvl + 1], ix[lvl])
            if live == depth:
                v[depth - 1] = jnp.where(hit, -jnp.inf, v[depth - 1])
    return jnp.concatenate(out_v, axis=0), jnp.concatenate(out_i, axis=0)


def _topk_pairs(va, vb, ia, ib):
    K, tokens = va.shape
    assert K == 2 * SUBLANES
    row = lax.broadcasted_iota(jnp.int32, (SUBLANES, tokens), 0).astype(jnp.float32)
    lo_v, lo_e = va[:SUBLANES, :], ia[:SUBLANES, :] * PEER_N_KEYS
    vals, flat, expert = [], [], []
    for j in range(K):
        v = lo_v + vb[j:j + 1, :]
        limit = K // (j + 1)
        vals.append(v if limit >= SUBLANES else jnp.where(row < limit, v, -jnp.inf))
        flat.append(row * K + float(j))
        expert.append(lo_e + ib[j:j + 1, :])
    hi_v = va[SUBLANES:, :] + vb[0:1, :]
    hi_flat = (row + SUBLANES) * K
    hi_e = ia[SUBLANES:, :] * PEER_N_KEYS + ib[0:1, :]
    best, picked = [], []
    for r in range(K):
        m = jnp.max(jnp.maximum(vals[0], hi_v), axis=0, keepdims=True)
        f = jnp.min(jnp.minimum(jnp.where(vals[0] == m, flat[0], float(K * K)),
                                jnp.where(hi_v == m, hi_flat, float(K * K))), axis=0, keepdims=True)
        hit_lo, hit_hi = flat[0] == f, hi_flat == f
        best.append(m)
        picked.append(jnp.max(jnp.maximum(jnp.where(hit_lo, expert[0], -1.0), jnp.where(hit_hi, hi_e, -1.0)),
                              axis=0, keepdims=True))
        for lvl in range(K - r - 1):
            vals[lvl] = jnp.where(hit_lo, vals[lvl + 1], vals[lvl])
            flat[lvl] = jnp.where(hit_lo, flat[lvl + 1], flat[lvl])
            expert[lvl] = jnp.where(hit_lo, expert[lvl + 1], expert[lvl])
        hi_v = jnp.where(hit_hi, -jnp.inf, hi_v)
    return jnp.concatenate(best, axis=0), jnp.concatenate(picked, axis=0)


def _route_tokens(h, wq_ref, sk_ref):
    q = _bdot(h, wq_ref[...])
    experts, gates = [], []
    for hd in range(PEER_HEADS):
        vals, ids = [], []
        for p in range(2):
            c0 = (hd * 2 + p) * PEER_HALF
            s_t = lax.dot_general(sk_ref[hd, p], q[:, c0:c0 + PEER_HALF], _NT,
                                  preferred_element_type=jnp.float32)
            v, i = _topk_keys(s_t, PEER_TOPK)
            vals.append(v)
            ids.append(i)
        best, expert = _topk_pairs(vals[0], vals[1], ids[0], ids[1])
        experts.append(expert * PACKED_TILES)
        e = jnp.exp(best - best[0:1, :])
        gates.append(e / jnp.sum(e, axis=0, keepdims=True))
    return jnp.concatenate(experts, axis=0).T.astype(jnp.int32), jnp.concatenate(gates, axis=0).T


def _pack_kernel(w_ref, o_ref):
    def bf16_bits(v):
        return lax.bitcast_convert_type(v.astype(jnp.bfloat16).astype(jnp.float32), jnp.uint32)

    for s in range(PACKED_TILES):
        lo = bf16_bits(w_ref[:, (2 * s) * LANES:(2 * s + 1) * LANES])
        hi = bf16_bits(w_ref[:, (2 * s + 1) * LANES:(2 * s + 2) * LANES])
        o_ref[:, s * LANES:(s + 1) * LANES] = hi | (lo >> 16)


def _pack_table(w):
    experts, D = w.shape
    words = pl.pallas_call(
        _pack_kernel,
        grid=(experts // PACK_ROWS,),
        in_specs=[pl.BlockSpec((PACK_ROWS, D), lambda i: (i, 0))],
        out_specs=pl.BlockSpec((PACK_ROWS, D // 2), lambda i: (i, 0)),
        out_shape=jax.ShapeDtypeStruct((experts, D // 2), jnp.uint32),
        compiler_params=pltpu.CompilerParams(dimension_semantics=("arbitrary",)),
        name="pack",
    )(w)
    return words.reshape(experts * PACKED_TILES, LANES)


def _offset_reader(off_ref, t):
    windows = [off_ref.at[pl.ds(t * PEER_SEL + s, OFFSET_WINDOW)] for s in range(0, PEER_SEL, OFFSET_WINDOW)]
    return lambda j: windows[j // OFFSET_WINDOW][j % OFFSET_WINDOW]


def _table_tile(tab, offset):
    words = tab[pl.ds(pl.multiple_of(offset, PACKED_TILES), PACKED_TILES), :]
    return pltpu.bitcast(words, jnp.bfloat16).astype(jnp.float32)


def _act_kernel(off_ref, h_ref, gate_ref, tab, w_ref, prod, act):
    rows = h_ref.shape[0]

    def products(t0):
        hs = [h_ref[t0 + g] for g in range(ACT_GROUP)]
        offs = [_offset_reader(off_ref, t0 + g) for g in range(ACT_GROUP)]
        for j in range(PEER_SEL):
            for g in range(ACT_GROUP):
                p = _table_tile(tab, offs[g](j)) * hs[g]
                prod[g, j * PACKED_TILES:(j + 1) * PACKED_TILES, :] = p[:PACKED_TILES] + p[PACKED_TILES:]

    def reduce(t0):
        for g in range(ACT_GROUP):
            folded = prod[g, pl.ds(0, PEER_SEL, stride=PACKED_TILES), :]
            for r in range(1, PACKED_TILES):
                folded = folded + prod[g, pl.ds(r, PEER_SEL, stride=PACKED_TILES), :]
            act[pl.ds(t0 + g, 1), :] = jnp.sum(folded.T, axis=0, keepdims=True)

    prod[...] = jnp.zeros(prod.shape, jnp.float32)

    def group(i, carry):
        reduce(jnp.maximum(i - 1, 0) * ACT_GROUP)
        products(i * ACT_GROUP)
        return carry

    lax.fori_loop(0, rows // ACT_GROUP, group, 0)
    reduce(rows - ACT_GROUP)
    w_ref[...] = gate_ref[...] * jax.nn.gelu(act[...])


def _combine_kernel(off_ref, w_ref, x_ref, g_ref, tab, o_ref, splat):
    rows = x_ref.shape[0]

    def spread(t):
        return jnp.broadcast_to(w_ref[pl.ds(t, 1), :], (LANES, PEER_SEL)).T

    for g in range(COMBINE_GROUP):
        splat[g] = spread(g)

    def group(i, carry):
        t0 = i * COMBINE_GROUP
        nxt = [spread(jnp.minimum(t0 + COMBINE_GROUP + g, rows - 1)) for g in range(COMBINE_GROUP)]
        offs = [_offset_reader(off_ref, t0 + g) for g in range(COMBINE_GROUP)]
        acc = [[x_ref[t0 + g]] + [jnp.zeros((ROW_TILES, LANES), jnp.float32)] * (COMBINE_CHAINS - 1)
               for g in range(COMBINE_GROUP)]
        for j in range(PEER_SEL):
            c = j % COMBINE_CHAINS
            for g in range(COMBINE_GROUP):
                w = jnp.broadcast_to(splat[g, j:j + 1, :], (ROW_TILES, LANES))
                acc[g][c] = acc[g][c] + w * _table_tile(tab, offs[g](j))
        for g in range(COMBINE_GROUP):
            o_ref[t0 + g] = sum(acc[g])
            splat[g] = nxt[g]
        return carry

    lax.fori_loop(0, rows // COMBINE_GROUP, group, 0)
    x = o_ref[...]
    ms = jnp.sum(jnp.sum(x * x, axis=2, keepdims=True), axis=1, keepdims=True) / D_MODEL
    o_ref[...] = x * lax.rsqrt(ms + EPS) * g_ref[...]


def _gather_specs(rows, tab):
    flat = pl.BlockSpec((rows * PEER_SEL,), lambda i: (i,), memory_space=pltpu.SMEM)
    tiles = pl.BlockSpec((rows, ROW_TILES, LANES), lambda i: (i, 0, 0))
    sel = pl.BlockSpec((rows, PEER_SEL), lambda i: (i, 0))
    table = pl.BlockSpec(tab.shape, lambda i: (0, 0), pipeline_mode=pl.Buffered(1))
    return flat, tiles, sel, table


def _act(off_flat, h3, gate, tab):
    T = h3.shape[0]
    rows = min(GATHER_ROWS, T)
    flat, tiles, sel, table = _gather_specs(rows, tab)
    prod = pltpu.VMEM((ACT_GROUP, PEER_SEL * PACKED_TILES, LANES), jnp.float32)
    return pl.pallas_call(
        _act_kernel,
        grid=(T // rows,),
        in_specs=[flat, tiles, sel, table],
        out_specs=sel,
        out_shape=jax.ShapeDtypeStruct((T, PEER_SEL), jnp.float32),
        scratch_shapes=[prod, pltpu.VMEM((rows, PEER_SEL), jnp.float32)],
        compiler_params=pltpu.CompilerParams(dimension_semantics=("arbitrary",),
                                             vmem_limit_bytes=VMEM_LIMIT),
        name="act",
    )(off_flat, h3, gate, tab)


def _combine(off_flat, w, x3, g3, tab):
    T = x3.shape[0]
    rows = min(GATHER_ROWS, T)
    flat, tiles, sel, table = _gather_specs(rows, tab)
    return pl.pallas_call(
        _combine_kernel,
        grid=(T // rows,),
        in_specs=[flat, sel, tiles, pl.BlockSpec((1, ROW_TILES, LANES), lambda i: (0, 0, 0)), table],
        out_specs=tiles,
        out_shape=jax.ShapeDtypeStruct(x3.shape, jnp.float32),
        scratch_shapes=[pltpu.VMEM((COMBINE_GROUP, PEER_SEL, LANES), jnp.float32)],
        compiler_params=pltpu.CompilerParams(dimension_semantics=("arbitrary",),
                                             vmem_limit_bytes=VMEM_LIMIT),
        name="combine",
    )(off_flat, w, x3, g3, tab)


def kernel(x, mix_norm_g, w_in, pool_w, pool_scale, w_branch_a, conv_w, conv_b, conv_ln_g, conv_ln_b, w_branch_b, gate_b, w_out, ffn_norm_g, peer_w_q, peer_sub_keys, peer_u, peer_v, final_norm_g):
    B, S, D = x.shape
    T = B * S
    assert mix_norm_g.shape[0] == 1, "the combine kernel fuses the final RMSNorm, so exactly one layer"
    bf = lambda w: w[0].astype(jnp.bfloat16)
    x1, h, off, gate = _mixer(x, mix_norm_g[0], bf(w_in), bf(pool_w), pool_scale[0], bf(w_branch_a), conv_w[0],
                              conv_b[0], conv_ln_g[0], conv_ln_b[0], bf(w_branch_b), gate_b[0], bf(w_out),
                              ffn_norm_g[0], bf(peer_w_q), peer_sub_keys[0])
    off_flat = off.reshape(T * PEER_SEL)
    w = _act(off_flat, h.reshape(T, ROW_TILES, LANES), gate.reshape(T, PEER_SEL), _pack_table(peer_u[0]))
    out = _combine(off_flat, w, x1.reshape(T, ROW_TILES, LANES),
                   final_norm_g.reshape(1, ROW_TILES, LANES), _pack_table(peer_v[0]))
    return out.reshape(B, S, D)
```
